```python
import jax, jax.numpy as jnp
from jax import lax
import numpy as np

D_MODEL = 1024
BATCH = 4
SEQ = 8192
DEPTH = 2
DEC_BATCH = 2
DEC_SEQ = 16384
PAST_LEN = 128

N_META = 16
GRID_W = 64
N_MIXERS = 2
N_SSD_LAYERS = (DEPTH + 1) // 2
N_ATTN_LAYERS = DEPTH // 2
RMS_EPS = 1e-6

SSD_EXPAND = 2
D_INNER = SSD_EXPAND * D_MODEL
SSD_HEAD_DIM = 64
SSD_HEADS = D_INNER // SSD_HEAD_DIM
SSD_GROUPS = 4
SSD_STATE = 128
CONV_W = 5
CONV_DIM = D_INNER + 2 * SSD_GROUPS * SSD_STATE
SSD_IN_DIM = D_INNER + CONV_DIM + 2 * SSD_HEADS
CHUNK = 128
META_PAD = CHUNK - N_META

ATTN_HEAD_DIM = 128
N_Q_HEADS = D_MODEL // ATTN_HEAD_DIM
N_KV_HEADS = 2
KV_REP = N_Q_HEADS // N_KV_HEADS
Q_WIDTH = N_Q_HEADS * ATTN_HEAD_DIM
KV_WIDTH = N_KV_HEADS * ATTN_HEAD_DIM
QKV_DIM = Q_WIDTH + 2 * KV_WIDTH
Q_BLOCK = 128
ATTN_SCALE = ATTN_HEAD_DIM ** -0.5
ROPE_THETA = 10000.0
ROPE_AXIS_DIM = ATTN_HEAD_DIM // 2
ROPE_FREQS = ROPE_AXIS_DIM // 2

PEER_HEADS = 8
N_KEYS = 128
N_EXPERTS = N_KEYS * N_KEYS
PEER_KEY_DIM = 256
PEER_HALF = PEER_KEY_DIM // 2
PEER_TOPK = 16
PEER_BLOCK = 128

kernel_name = 'bidir_hybrid_ssd_axialgqa_peer'


def rms_norm(x, g):
    xf = x.astype(jnp.float32)
    y = xf * lax.rsqrt(jnp.mean(xf * xf, axis=-1, keepdims=True) + RMS_EPS)
    return (y * g.astype(jnp.float32)).astype(x.dtype)


def ssd_chunked(x, dt, a, Bm, Cm):
    b, T, H, P = x.shape
    G, N = SSD_GROUPS, SSD_STATE
    R = H // G
    nc = T // CHUNK
    xdt = (x * dt[..., None]).reshape(b, nc, CHUNK, G, R, P)
    adt = (dt * a).reshape(b, nc, CHUNK, G, R).transpose(0, 1, 3, 4, 2)
    cs = jnp.cumsum(adt, axis=-1)
    Bc = Bm.reshape(b, nc, CHUNK, G, N)
    Cc = Cm.reshape(b, nc, CHUNK, G, N)
    lower = jnp.tril(jnp.ones((CHUNK, CHUNK), dtype=bool))
    seg = jnp.where(lower, cs[..., :, None] - cs[..., None, :], -jnp.inf)
    cb = jnp.einsum('bclgn,bcsgn->bcgls', Cc, Bc)
    w = cb[:, :, :, None] * jnp.exp(seg)
    y_diag = jnp.einsum('bcgrls,bcsgrp->bclgrp', w, xdt)
    decay_to_end = jnp.exp(cs[..., -1:] - cs)
    states = jnp.einsum('bclgn,bcgrl,bclgrp->bcgrpn', Bc, decay_to_end, xdt)
    chunk_decay = jnp.exp(cs[..., -1])

    def carry_state(h, inp):
        st, dec = inp
        return h * dec[..., None, None] + st, h

    h0 = jnp.zeros((b, G, R, P, N), jnp.float32)
    _, prev = lax.scan(carry_state, h0, (jnp.moveaxis(states, 1, 0), jnp.moveaxis(chunk_decay, 1, 0)))
    prev = jnp.moveaxis(prev, 0, 1)
    y_off = jnp.einsum('bclgn,bcgrpn,bcgrl->bclgrp', Cc, prev, jnp.exp(cs))
    return (y_diag + y_off).reshape(b, T, H, P)


def ssd_mixer(h, w_in, conv_w, conv_b, dt_bias, a_log, d_skip, gate_norm_g, w_out):
    b, L, _ = h.shape
    f32 = jnp.float32
    proj = h @ w_in
    z = proj[..., :D_INNER]
    xbc = proj[..., D_INNER:D_INNER + CONV_DIM]
    dt_raw = proj[..., D_INNER + CONV_DIM:]
    half = (CONV_W - 1) // 2
    xbc = lax.conv_general_dilated(xbc, conv_w[:, None, :], window_strides=(1,), padding=[(half, half)],
                                   dimension_numbers=('NWC', 'WIO', 'NWC'), feature_group_count=CONV_DIM)
    xbc = jax.nn.silu(xbc + conv_b)
    xs = xbc[..., :D_INNER].reshape(b, L, SSD_HEADS, SSD_HEAD_DIM)
    Bm = xbc[..., D_INNER:D_INNER + SSD_GROUPS * SSD_STATE].reshape(b, L, SSD_GROUPS, SSD_STATE)
    Cm = xbc[..., D_INNER + SSD_GROUPS * SSD_STATE:].reshape(b, L, SSD_GROUPS, SSD_STATE)
    dt = jax.nn.softplus((dt_raw.reshape(b, L, 2, SSD_HEADS) + dt_bias).astype(f32))
    a = -jnp.exp(a_log.astype(f32))

    def front(t):
        return jnp.pad(t.astype(f32), [(0, 0), (META_PAD, 0)] + [(0, 0)] * (t.ndim - 2))

    def flip(t):
        return jnp.flip(t, axis=1)

    xp, Bp, Cp, dtp = front(xs), front(Bm), front(Cm), front(dt)
    y_fwd = ssd_chunked(xp, dtp[:, :, 0], a[0], Bp, Cp)
    y_bwd = flip(ssd_chunked(flip(xp), flip(dtp[:, :, 1]), a[1], flip(Bp), flip(Cp)))
    y = (y_fwd + y_bwd)[:, META_PAD:] + d_skip.astype(f32)[:, None] * xs.astype(f32)
    y = y.reshape(b, L, D_INNER).astype(h.dtype)
    y = rms_norm(y * jax.nn.silu(z), gate_norm_g)
    return y @ w_out


def axial_rope_tables(n_tokens):
    rows = n_tokens // GRID_W
    tok_row = jnp.repeat(jnp.arange(rows), GRID_W)
    tok_col = jnp.tile(jnp.arange(GRID_W), rows)
    row = jnp.concatenate([-jnp.ones((N_META,), jnp.int32), tok_row.astype(jnp.int32)]).astype(jnp.float32)
    col = jnp.concatenate([jnp.arange(N_META, dtype=jnp.int32), tok_col.astype(jnp.int32)]).astype(jnp.float32)
    inv_freq = ROPE_THETA ** (-jnp.arange(ROPE_FREQS, dtype=jnp.float32) / ROPE_FREQS)
    ang = jnp.concatenate([row[:, None] * inv_freq, col[:, None] * inv_freq], axis=-1)
    ang = jnp.concatenate([ang, ang], axis=-1)
    return jnp.cos(ang), jnp.sin(ang)


def apply_rope(x, cos, sin):
    xf = x.astype(jnp.float32)
    x1, x2 = xf[..., :ATTN_HEAD_DIM // 2], xf[..., ATTN_HEAD_DIM // 2:]
    rot = jnp.concatenate([-x2, x1], axis=-1)
    return (xf * cos[:, None] + rot * sin[:, None]).astype(x.dtype)


def attention_mixer(h, w_qkv, q_norm_g, k_norm_g, w_o):
    b, L, _ = h.shape
    n_tokens = L - N_META
    qkv = h @ w_qkv
    q = qkv[..., :Q_WIDTH].reshape(b, L, N_Q_HEADS, ATTN_HEAD_DIM)
    k = qkv[..., Q_WIDTH:Q_WIDTH + KV_WIDTH].reshape(b, L, N_KV_HEADS, ATTN_HEAD_DIM)
    v = qkv[..., Q_WIDTH + KV_WIDTH:].reshape(b, L, N_KV_HEADS, ATTN_HEAD_DIM)
    cos, sin = axial_rope_tables(n_tokens)
    q = apply_rope(rms_norm(q, q_norm_g), cos, sin) * ATTN_SCALE
    k = apply_rope(rms_norm(k, k_norm_g), cos, sin)
    q = q.reshape(b, L, N_KV_HEADS, KV_REP, ATTN_HEAD_DIM)

    def attend(qb):
        s = jnp.einsum('bqgrd,bkgd->bgrqk', qb, k).astype(jnp.float32)
        p = jax.nn.softmax(s, axis=-1).astype(v.dtype)
        return jnp.einsum('bgrqk,bkgd->bqgrd', p, v)

    o_meta = attend(q[:, :N_META]).reshape(b, N_META, Q_WIDTH)
    qb = q[:, N_META:].reshape(b, n_tokens // Q_BLOCK, Q_BLOCK, N_KV_HEADS, KV_REP, ATTN_HEAD_DIM)
    o_tok = lax.map(attend, jnp.moveaxis(qb, 1, 0))
    o_tok = jnp.moveaxis(o_tok, 0, 1).reshape(b, n_tokens, Q_WIDTH)
    o = jnp.concatenate([o_meta, o_tok], axis=1)
    return o @ w_o


def peer(h, w_q, sub_keys, u, v):
    b, L, D = h.shape
    T = b * L
    n_blk = -(-T // PEER_BLOCK)
    xf = jnp.pad(h.reshape(T, D), ((0, n_blk * PEER_BLOCK - T), (0, 0))).reshape(n_blk, PEER_BLOCK, D)

    def retrieve(xb):
        q = (xb @ w_q).reshape(PEER_BLOCK, PEER_HEADS, 2, PEER_HALF)
        s = jnp.einsum('thcd,hckd->thck', q, sub_keys).astype(jnp.float32)
        sv, si = lax.top_k(s, PEER_TOPK)
        cand = (sv[..., 0, :, None] + sv[..., 1, None, :]).reshape(PEER_BLOCK, PEER_HEADS, PEER_TOPK * PEER_TOPK)
        cand_idx = (si[..., 0, :, None] * N_KEYS + si[..., 1, None, :]).reshape(PEER_BLOCK, PEER_HEADS, PEER_TOPK * PEER_TOPK)
        top, pos = lax.top_k(cand, PEER_TOPK)
        eidx = jnp.take_along_axis(cand_idx, pos, axis=-1)
        g = jax.nn.softmax(top, axis=-1).astype(xb.dtype)
        act = jax.nn.gelu(jnp.einsum('thkd,td->thk', u[eidx], xb), approximate=False)
        return jnp.einsum('thk,thkd->td', g * act, v[eidx])

    out = lax.map(retrieve, xf).reshape(n_blk * PEER_BLOCK, D)[:T]
    return out.reshape(b, L, D)


def trunk(x, meta_tokens, norm_mix_g, norm_ffn_g, ssd_w_in, ssd_conv_w, ssd_conv_b, ssd_dt_bias, ssd_a_log,
          ssd_d_skip, ssd_gate_norm_g, ssd_w_out, attn_w_qkv, attn_q_norm_g, attn_k_norm_g, attn_w_o,
          peer_w_q, peer_sub_keys, peer_u, peer_v):
    b = x.shape[0]
    meta = jnp.broadcast_to(meta_tokens[None].astype(x.dtype), (b, N_META, D_MODEL))
    h = jnp.concatenate([meta, x], axis=1)
    for i in range(DEPTH):
        hn = rms_norm(h, norm_mix_g[i])
        j = i // N_MIXERS
        if i % N_MIXERS == 0:
            h = h + ssd_mixer(hn, ssd_w_in[j], ssd_conv_w[j], ssd_conv_b[j], ssd_dt_bias[j], ssd_a_log[j],
                              ssd_d_skip[j], ssd_gate_norm_g[j], ssd_w_out[j])
        else:
            h = h + attention_mixer(hn, attn_w_qkv[j], attn_q_norm_g[j], attn_k_norm_g[j], attn_w_o[j])
        h = h + peer(rms_norm(h, norm_ffn_g[i]), peer_w_q[i], peer_sub_keys[i], peer_u[i], peer_v[i])
    return h[:, N_META:]


def setup_inputs(seed: int = 0) -> dict:
    key = jax.random.key(seed)
    ks = jax.random.split(key, 24)
    nrm = jax.random.normal
    dt0 = jnp.exp(jax.random.uniform(ks[7], (N_SSD_LAYERS, 2, SSD_HEADS), minval=float(np.log(1e-3)), maxval=float(np.log(1e-1))))
    return {
        'x_prompt': nrm(ks[0], (BATCH, SEQ, D_MODEL), jnp.float32),
        'x_sample': nrm(ks[1], (DEC_BATCH, DEC_SEQ, D_MODEL), jnp.float32),
        'meta_tokens': nrm(ks[2], (N_META, D_MODEL), jnp.float32),
        'norm_mix_g': 1.0 + 0.02 * nrm(ks[3], (DEPTH, D_MODEL), jnp.float32),
        'norm_ffn_g': 1.0 + 0.02 * nrm(ks[4], (DEPTH, D_MODEL), jnp.float32),
        'ssd_w_in': nrm(ks[5], (N_SSD_LAYERS, D_MODEL, SSD_IN_DIM), jnp.float32) * D_MODEL ** -0.5,
        'ssd_conv_w': nrm(ks[6], (N_SSD_LAYERS, CONV_W, CONV_DIM), jnp.float32) * CONV_W ** -0.5,
        'ssd_conv_b': 0.02 * nrm(ks[8], (N_SSD_LAYERS, CONV_DIM), jnp.float32),
        'ssd_dt_bias': dt0 + jnp.log(-jnp.expm1(-dt0)),
        'ssd_a_log': jnp.log(jax.random.uniform(ks[9], (N_SSD_LAYERS, 2, SSD_HEADS), minval=1.0, maxval=16.0)),
        'ssd_d_skip': 1.0 + 0.02 * nrm(ks[10], (N_SSD_LAYERS, SSD_HEADS), jnp.float32),
        'ssd_gate_norm_g': 1.0 + 0.02 * nrm(ks[11], (N_SSD_LAYERS, D_INNER), jnp.float32),
        'ssd_w_out': nrm(ks[12], (N_SSD_LAYERS, D_INNER, D_MODEL), jnp.float32) * D_INNER ** -0.5,
        'attn_w_qkv': nrm(ks[13], (N_ATTN_LAYERS, D_MODEL, QKV_DIM), jnp.float32) * D_MODEL ** -0.5,
        'attn_q_norm_g': 1.0 + 0.02 * nrm(ks[14], (N_ATTN_LAYERS, ATTN_HEAD_DIM), jnp.float32),
        'attn_k_norm_g': 1.0 + 0.02 * nrm(ks[15], (N_ATTN_LAYERS, ATTN_HEAD_DIM), jnp.float32),
        'attn_w_o': nrm(ks[16], (N_ATTN_LAYERS, Q_WIDTH, D_MODEL), jnp.float32) * Q_WIDTH ** -0.5,
        'peer_w_q': nrm(ks[17], (DEPTH, D_MODEL, PEER_HEADS * PEER_KEY_DIM), jnp.float32) * D_MODEL ** -0.5,
        'peer_sub_keys': nrm(ks[18], (DEPTH, PEER_HEADS, 2, N_KEYS, PEER_HALF), jnp.float32) * PEER_HALF ** -0.5,
        'peer_u': nrm(ks[19], (DEPTH, N_EXPERTS, D_MODEL), jnp.float32) * D_MODEL ** -0.5,
        'peer_v': nrm(ks[20], (DEPTH, N_EXPERTS, D_MODEL), jnp.float32) * D_MODEL ** -0.5,
    }


def reference(x_prompt, x_sample, meta_tokens, norm_mix_g, norm_ffn_g, ssd_w_in, ssd_conv_w, ssd_conv_b,
              ssd_dt_bias, ssd_a_log, ssd_d_skip, ssd_gate_norm_g, ssd_w_out, attn_w_qkv, attn_q_norm_g,
              attn_k_norm_g, attn_w_o, peer_w_q, peer_sub_keys, peer_u, peer_v):
    weights = (meta_tokens, norm_mix_g, norm_ffn_g, ssd_w_in, ssd_conv_w, ssd_conv_b, ssd_dt_bias, ssd_a_log,
               ssd_d_skip, ssd_gate_norm_g, ssd_w_out, attn_w_qkv, attn_q_norm_g, attn_k_norm_g, attn_w_o,
               peer_w_q, peer_sub_keys, peer_u, peer_v)
    y_prompt = trunk(x_prompt, *weights)
    y_sample = trunk(x_sample, *weights)
    return (y_prompt, y_sample)
```

```python
import functools

import jax
import jax.numpy as jnp
import numpy as np
from jax import lax
from jax.experimental import pallas as pl
from jax.experimental.pallas import tpu as pltpu

LANES = 128
SUBLANES = 8
VREG_ELEMS = LANES * SUBLANES
V7X_VMEM_BYTES = 64 * 1024 * 1024

D_MODEL = 1024
N_META = 16
FRONT = 128
META_PAD = FRONT - N_META
RMS_EPS = 1e-6

PEER_HEADS = 8
N_KEYS = 128
PEER_HALF = 128
PEER_TOPK = 16
N_ASSIGN = PEER_HEADS * PEER_TOPK
N_EXPERTS = N_KEYS * N_KEYS
D_CHUNKS = D_MODEL // LANES
HALF_CHUNKS = D_CHUNKS // 2
PAIR_ROW_OF_CHUNK = tuple(2 * (c % HALF_CHUNKS) + c // HALF_CHUNKS for c in range(D_CHUNKS))

NEG_INF = float("-inf")


def _vmem_limit(resident_bytes):
    return int(min(V7X_VMEM_BYTES - 8 * 1024 * 1024, resident_bytes + 20 * 1024 * 1024))


def _rms(x, g):
    return x * lax.rsqrt(jnp.mean(x * x, axis=-1, keepdims=True) + RMS_EPS) * g


PEER_Q_ROWS = 256


def _top16_rows(s, n_rows):
    rid = lax.broadcasted_iota(jnp.int32, s.shape, 0)
    vals, ids = [], []
    for _ in range(PEER_TOPK):
        m = jnp.max(s, axis=0, keepdims=True)
        i = jnp.min(jnp.where(s == m, rid, n_rows), axis=0, keepdims=True)
        vals.append(m)
        ids.append(i)
        s = jnp.where(rid == i, NEG_INF, s)
    return jnp.concatenate(vals, axis=0), jnp.concatenate(ids, axis=0)


def _peer_topk_kernel(h_ref, g_ref, wqt_ref, keys_ref, x3_ref, eid_ref, gate_ref):
    rows = h_ref.shape[0]
    x = _rms(h_ref[...], g_ref[...])
    for c in range(D_CHUNKS):
        x3_ref[pl.ds(PAIR_ROW_OF_CHUNK[c], rows, stride=D_CHUNKS), :] = x[:, c * LANES:(c + 1) * LANES]
    xb = x.astype(jnp.bfloat16)
    eids, gates = [], []
    for hd in range(PEER_HEADS):
        sv, si = [], []
        for c in range(2):
            j = hd * 2 + c
            qt = lax.dot_general(wqt_ref[j * PEER_HALF:(j + 1) * PEER_HALF, :], xb,
                                 (((1,), (1,)), ((), ())), preferred_element_type=jnp.float32)
            st = jnp.dot(keys_ref[j], qt.astype(jnp.bfloat16), preferred_element_type=jnp.float32)
            v, i = _top16_rows(st, N_KEYS)
            sv.append(v)
            si.append(i)
        cand = jnp.concatenate([sv[0][i:i + 1] + sv[1] for i in range(PEER_TOPK)], axis=0)
        cidx = jnp.concatenate([si[0][i:i + 1] * N_KEYS + si[1] for i in range(PEER_TOPK)], axis=0)
        pos = lax.broadcasted_iota(jnp.int32, cand.shape, 0)
        tops, sel = [], []
        for _ in range(PEER_TOPK):
            m = jnp.max(cand, axis=0, keepdims=True)
            p = jnp.min(jnp.where(cand == m, pos, PEER_TOPK * PEER_TOPK), axis=0, keepdims=True)
            hit = pos == p
            tops.append(m)
            sel.append(jnp.sum(jnp.where(hit, cidx, 0), axis=0, keepdims=True))
            cand = jnp.where(hit, NEG_INF, cand)
        top = jnp.concatenate(tops, axis=0)
        e = jnp.exp(top - top[0:1])
        gates.append(e / jnp.sum(e, axis=0, keepdims=True))
        eids.append(jnp.concatenate(sel, axis=0))
    gate_t = jnp.concatenate(gates, axis=0)
    eid_t = jnp.concatenate(eids, axis=0)
    for b in range(rows // LANES):
        sl = slice(b * LANES, (b + 1) * LANES)
        gate_ref[sl, :] = gate_t[:, sl].T
        eid_ref[sl, :] = eid_t[:, sl].T * HALF_CHUNKS


def peer_topk(h, g, wqt, keys):
    rows = h.shape[0]
    tb = PEER_Q_ROWS
    assert rows % tb == 0
    const2 = lambda i: (0, 0)
    return pl.pallas_call(
        _peer_topk_kernel,
        grid=(rows // tb,),
        in_specs=[
            pl.BlockSpec((tb, D_MODEL), lambda i: (i, 0)),
            pl.BlockSpec((1, D_MODEL), const2),
            pl.BlockSpec(wqt.shape, const2, pipeline_mode=pl.Buffered(1)),
            pl.BlockSpec(keys.shape, lambda i: (0, 0, 0), pipeline_mode=pl.Buffered(1)),
        ],
        out_specs=[
            pl.BlockSpec((tb * D_CHUNKS, LANES), lambda i: (i, 0)),
            pl.BlockSpec((tb, N_ASSIGN), lambda i: (i, 0)),
            pl.BlockSpec((tb, N_ASSIGN), lambda i: (i, 0)),
        ],
        out_shape=[
            jax.ShapeDtypeStruct((rows * D_CHUNKS, LANES), jnp.float32),
            jax.ShapeDtypeStruct((rows, N_ASSIGN), jnp.int32),
            jax.ShapeDtypeStruct((rows, N_ASSIGN), jnp.float32),
        ],
        compiler_params=pltpu.CompilerParams(
            dimension_semantics=("arbitrary",),
            vmem_limit_bytes=_vmem_limit(wqt.size * 2 + keys.size * 2)),
        name="peer_topk",
    )(h, g, wqt, keys)


PEER_ROWS = 128
PEER_GROUP = 4


def _tile_scratch():
    return [pltpu.VMEM((N_ASSIGN * HALF_CHUNKS, LANES), jnp.uint32) for _ in range(2 * PEER_GROUP)]


def pack_expert_table(w):
    b = lax.bitcast_convert_type(w.astype(jnp.bfloat16), jnp.uint16).astype(jnp.uint32)
    b = b.reshape(N_EXPERTS, 2, HALF_CHUNKS, LANES)
    return (b[:, 0] | (b[:, 1] << 16)).reshape(N_EXPERTS * HALF_CHUNKS, LANES)


def _gather_tile(eid_ref, t, tbl_ref, tile_ref):
    for a in range(N_ASSIGN):
        e4 = pl.multiple_of(eid_ref[t, a], HALF_CHUNKS)
        tile_ref[a * HALF_CHUNKS:(a + 1) * HALF_CHUNKS, :] = tbl_ref[pl.ds(e4, HALF_CHUNKS), :]


def _split3(x):
    hi = x.astype(jnp.bfloat16)
    r = x - hi.astype(jnp.float32)
    mid = r.astype(jnp.bfloat16)
    lo = (r - mid.astype(jnp.float32)).astype(jnp.bfloat16)
    return hi, mid, lo


def _dot3(x, m):
    return sum(jnp.dot(p, m, preferred_element_type=jnp.float32) for p in _split3(x))


def _diag_mask(shape):
    q = lax.broadcasted_iota(jnp.int32, shape, 0)
    n = lax.broadcasted_iota(jnp.int32, shape, 1)
    return (n % SUBLANES) == q


def _for_each_token(rows, gather, compute, tiles_a, tiles_b):
    group = len(tiles_a)
    n_groups = rows // group

    def gather_group(j, tiles):
        for k, tile in enumerate(tiles):
            gather(j * group + k, tile)

    def compute_group(j, tiles):
        for k, tile in enumerate(tiles):
            compute(j * group + k, tile)

    gather_group(0, tiles_a)

    def pair(i, carry):
        j = 2 * i
        compute_group(j, tiles_a)
        gather_group(j + 1, tiles_b)

        @pl.when(j + 1 < n_groups)
        def _():
            compute_group(j + 1, tiles_b)
            gather_group(jnp.minimum(j + 2, n_groups - 1), tiles_a)

        return carry

    lax.fori_loop(0, n_groups // 2, pair, 0)


def _peer_act_kernel(eid_ref, x3_ref, gate_ref, group_ref, tbl_ref, w_ref, part_ref, *tiles):
    rows = gate_ref.shape[0]
    mask = _diag_mask((SUBLANES, N_ASSIGN * SUBLANES))
    zeros = jnp.zeros((SUBLANES, LANES), jnp.bfloat16)

    def compute(t, tile_ref):
        xt = x3_ref[pl.ds(pl.multiple_of(t * SUBLANES, SUBLANES), SUBLANES), :].astype(jnp.bfloat16)
        lhs = jnp.concatenate([xt, zeros], axis=0)
        u = pltpu.bitcast(tile_ref[...], jnp.bfloat16)
        r = lax.dot_general(lhs, u, (((1,), (1,)), ((), ())), preferred_element_type=jnp.float32)
        part_ref[pl.ds(t, 1), :] = jnp.sum(jnp.where(mask, r[0:SUBLANES], 0.0), axis=0, keepdims=True)

    gather = lambda t, tile_ref: _gather_tile(eid_ref, t, tbl_ref, tile_ref)
    _for_each_token(rows, gather, compute, tiles[:PEER_GROUP], tiles[PEER_GROUP:])
    act = _dot3(part_ref[...], group_ref[...])
    gelu = 0.5 * act * (1.0 + lax.erf(act * np.float32(1.0 / np.sqrt(2.0))))
    w_ref[...] = gate_ref[...] * gelu


def _peer_out_kernel(eid_ref, w_ref, h_ref, expand_ref, tbl_ref, o_ref, wexp_ref, acc_ref, *tiles):
    rows = w_ref.shape[0]
    mask = _diag_mask((SUBLANES, N_ASSIGN * SUBLANES))
    wexp_ref[...] = _dot3(w_ref[...], expand_ref[...])

    def compute(t, tile_ref):
        wrow = jnp.where(mask, wexp_ref[pl.ds(t, 1), :], 0.0)
        hi = wrow.astype(jnp.bfloat16)
        lo = (wrow - hi.astype(jnp.float32)).astype(jnp.bfloat16)
        lhs = jnp.concatenate([hi, lo], axis=0)
        v = pltpu.bitcast(tile_ref[...], jnp.bfloat16)
        r = jnp.dot(lhs, v, preferred_element_type=jnp.float32)
        acc_ref[pl.ds(pl.multiple_of(t * SUBLANES, SUBLANES), SUBLANES), :] = r[0:SUBLANES] + r[SUBLANES:]

    gather = lambda t, tile_ref: _gather_tile(eid_ref, t, tbl_ref, tile_ref)
    _for_each_token(rows, gather, compute, tiles[:PEER_GROUP], tiles[PEER_GROUP:])
    for c in range(D_CHUNKS):
        sl = slice(c * LANES, (c + 1) * LANES)
        o_ref[:, sl] = h_ref[:, sl] + acc_ref[pl.ds(PAIR_ROW_OF_CHUNK[c], rows, stride=D_CHUNKS), :]


def _group_matrix():
    n = np.arange(N_ASSIGN * SUBLANES)
    return jnp.asarray((n[:, None] // SUBLANES) == np.arange(N_ASSIGN)[None, :], jnp.bfloat16)


def peer_act(eid, x3, gate, tbl):
    rows = eid.shape[0]
    tb = PEER_ROWS
    assert rows % tb == 0
    group = _group_matrix()
    return pl.pallas_call(
        _peer_act_kernel,
        grid=(rows // tb,),
        in_specs=[
            pl.BlockSpec((tb, N_ASSIGN), lambda i: (i, 0), memory_space=pltpu.SMEM),
            pl.BlockSpec((tb * D_CHUNKS, LANES), lambda i: (i, 0)),
            pl.BlockSpec((tb, N_ASSIGN), lambda i: (i, 0)),
            pl.BlockSpec(group.shape, lambda i: (0, 0)),
            pl.BlockSpec(tbl.shape, lambda i: (0, 0), pipeline_mode=pl.Buffered(1)),
        ],
        out_specs=pl.BlockSpec((tb, N_ASSIGN), lambda i: (i, 0)),
        out_shape=jax.ShapeDtypeStruct((rows, N_ASSIGN), jnp.float32),
        scratch_shapes=[pltpu.VMEM((tb, N_ASSIGN * SUBLANES), jnp.float32)] + _tile_scratch(),
        compiler_params=pltpu.CompilerParams(
            dimension_semantics=("arbitrary",), vmem_limit_bytes=_vmem_limit(tbl.size * 4)),
        name="peer_act",
    )(eid, x3, gate, group, tbl)


def peer_out(eid, w, h, tbl):
    rows = eid.shape[0]
    tb = PEER_ROWS
    assert rows % tb == 0
    expand = _group_matrix().T
    return pl.pallas_call(
        _peer_out_kernel,
        grid=(rows // tb,),
        in_specs=[
            pl.BlockSpec((tb, N_ASSIGN), lambda i: (i, 0), memory_space=pltpu.SMEM),
            pl.BlockSpec((tb, N_ASSIGN), lambda i: (i, 0)),
            pl.BlockSpec((tb, D_MODEL), lambda i: (i, 0)),
            pl.BlockSpec(expand.shape, lambda i: (0, 0)),
            pl.BlockSpec(tbl.shape, lambda i: (0, 0), pipeline_mode=pl.Buffered(1)),
        ],
        out_specs=pl.BlockSpec((tb, D_MODEL), lambda i: (i, 0)),
        out_shape=jax.ShapeDtypeStruct((rows, D_MODEL), jnp.float32),
        scratch_shapes=[
            pltpu.VMEM((tb, N_ASSIGN * SUBLANES), jnp.float32),
            pltpu.VMEM((tb * D_CHUNKS, LANES), jnp.float32),
        ] + _tile_scratch(),
        compiler_params=pltpu.CompilerParams(
            dimension_semantics=("arbitrary",), vmem_limit_bytes=_vmem_limit(tbl.size * 4)),
        name="peer_out",
    )(eid, w, h, expand, tbl)


def peer_layer(h, g, wqt, keys, u_tbl, v_tbl):
    x3, eid, gate = peer_topk(h, g, wqt, keys)
    w = peer_act(eid, x3, gate, u_tbl)
    return peer_out(eid, w, h, v_tbl)


D_INNER = 2048
SSD_HEAD_DIM = 64
SSD_HEADS = D_INNER // SSD_HEAD_DIM
SSD_GROUPS = 4
HEADS_PER_GROUP = SSD_HEADS // SSD_GROUPS
SSD_STATE = 128
CONV_W = 5
CONV_HALF = (CONV_W - 1) // 2
CONV_DIM = D_INNER + 2 * SSD_GROUPS * SSD_STATE
CHUNK = 128
HALO_ROWS = 16
DENSE_ROWS = 256


def _ssd_in_kernel(h_ref, g_ref, wz_ref, wx_ref, wdt_ref, wdtt_ref, z_ref, xbc_ref, dt_ref, dtt_ref):
    xb = _rms(h_ref[...], g_ref[...]).astype(jnp.bfloat16)
    z_ref[...] = jnp.dot(xb, wz_ref[...], preferred_element_type=jnp.float32).astype(z_ref.dtype)
    xbc_ref[...] = jnp.dot(xb, wx_ref[...], preferred_element_type=jnp.float32).astype(xbc_ref.dtype)
    dt_ref[...] = jnp.dot(xb, wdt_ref[...], preferred_element_type=jnp.float32)
    dtt_ref[...] = lax.dot_general(wdtt_ref[...], xb, (((1,), (1,)), ((), ())),
                                   preferred_element_type=jnp.float32)


def ssd_in(h, g, w_in):
    rows = h.shape[0]
    tm = DENSE_ROWS
    assert rows % tm == 0
    wb = w_in.astype(jnp.bfloat16)
    wz, wx, wdt = wb[:, :D_INNER], wb[:, D_INNER:D_INNER + CONV_DIM], wb[:, D_INNER + CONV_DIM:]
    n_dt = wdt.shape[1]
    once = lambda shape: pl.BlockSpec(shape, lambda i: (0,) * len(shape), pipeline_mode=pl.Buffered(1))
    return pl.pallas_call(
        _ssd_in_kernel,
        grid=(rows // tm,),
        in_specs=[
            pl.BlockSpec((tm, D_MODEL), lambda i: (i, 0)),
            once((1, D_MODEL)), once(wz.shape), once(wx.shape), once(wdt.shape), once((n_dt, D_MODEL)),
        ],
        out_specs=[
            pl.BlockSpec((tm, D_INNER), lambda i: (i, 0)),
            pl.BlockSpec((tm, CONV_DIM), lambda i: (i, 0)),
            pl.BlockSpec((tm, n_dt), lambda i: (i, 0)),
            pl.BlockSpec((n_dt, tm), lambda i: (0, i)),
        ],
        out_shape=[
            jax.ShapeDtypeStruct((rows, D_INNER), jnp.bfloat16),
            jax.ShapeDtypeStruct((rows, CONV_DIM), jnp.bfloat16),
            jax.ShapeDtypeStruct((rows, n_dt), jnp.float32),
            jax.ShapeDtypeStruct((n_dt, rows), jnp.float32),
        ],
        compiler_params=pltpu.CompilerParams(
            dimension_semantics=("arbitrary",), vmem_limit_bytes=_vmem_limit(wb.size * 2)),
        name="ssd_in",
    )(h, g, wz, wx, wdt, wdt.T)


def _dot2(x, m):
    hi = x.astype(jnp.bfloat16)
    lo = (x - hi.astype(jnp.float32)).astype(jnp.bfloat16)
    return (jnp.dot(hi, m, preferred_element_type=jnp.float32)
            + jnp.dot(lo, m, preferred_element_type=jnp.float32))


def _softplus(x):
    return jnp.maximum(x, 0.0) + jnp.log(1.0 + jnp.exp(-jnp.abs(x)))


def _silu(x):
    return x / (1.0 + jnp.exp(-x))


def _ssd_scan_kernel(first_ref, last_ref,
                     prev_ref, cur_ref, next_ref, dt_ref, dtt_ref,
                     convw_ref, convb_ref, bias_ref, biast_ref, alog_ref, alogt_ref, *rest,
                     reverse):
    if reverse:
        (yf_ref, z_ref, h_ref, dskip_ref, gn_ref, wout_ref, o_ref,
         ext_ref, xs_ref, bt_ref, c_ref, y_ref, state_ref) = rest
    else:
        o_ref, ext_ref, xs_ref, bt_ref, c_ref, y_ref, state_ref = rest
    ci = pl.program_id(0)
    n_chunks = pl.num_programs(0)
    chunk_id = (n_chunks - 1 - ci) if reverse else ci
    seq_first = first_ref[chunk_id] == 1
    seq_last = last_ref[chunk_id] == 1
    starts = seq_last if reverse else seq_first

    @pl.when(starts)
    def _():
        state_ref[...] = jnp.zeros_like(state_ref)

    keep_prev = jnp.where(seq_first, 0.0, 1.0)
    keep_next = jnp.where(seq_last, 0.0, 1.0)
    ext_ref[0:SUBLANES, :] = prev_ref[HALO_ROWS - SUBLANES:, :].astype(jnp.float32) * keep_prev
    ext_ref[SUBLANES:SUBLANES + CHUNK, :] = cur_ref[...].astype(jnp.float32)
    ext_ref[SUBLANES + CHUNK:, :] = next_ref[0:SUBLANES, :].astype(jnp.float32) * keep_next
    strip = 4 * LANES
    for s0 in range(0, CONV_DIM, strip):
        acc = jnp.zeros((CHUNK, strip), jnp.float32) + convb_ref[:, s0:s0 + strip]
        for k in range(CONV_W):
            r0 = SUBLANES - CONV_HALF + k
            acc = acc + ext_ref[r0:r0 + CHUNK, s0:s0 + strip] * convw_ref[k:k + 1, s0:s0 + strip]
        act = _silu(acc)
        if s0 < D_INNER:
            xs_ref[:, s0:s0 + strip] = act
        elif s0 < D_INNER + SSD_GROUPS * SSD_STATE:
            for g in range(strip // SSD_STATE):
                gi = (s0 - D_INNER) // SSD_STATE + g
                bt_ref[gi] = act[:, g * SSD_STATE:(g + 1) * SSD_STATE].T.astype(jnp.bfloat16)
        else:
            b0 = s0 - D_INNER - SSD_GROUPS * SSD_STATE
            c_ref[:, b0:b0 + strip] = act.astype(jnp.bfloat16)

    d0 = SSD_HEADS if reverse else 0
    row = lax.broadcasted_iota(jnp.int32, (CHUNK, SSD_HEADS), 0)
    col = lax.broadcasted_iota(jnp.int32, (SSD_HEADS, CHUNK), 1)
    valid = jnp.logical_or(jnp.logical_not(seq_first), row >= META_PAD)
    valid_t = jnp.logical_or(jnp.logical_not(seq_first), col >= META_PAD)
    dt = jnp.where(valid, _softplus(dt_ref[:, d0:d0 + SSD_HEADS] + bias_ref[...]), 0.0)
    dt_t = jnp.where(valid_t, _softplus(dtt_ref[...] + biast_ref[...]), 0.0)
    adt = dt * -jnp.exp(alog_ref[...])
    adt_t = dt_t * -jnp.exp(alogt_ref[...])
    li = lax.broadcasted_iota(jnp.int32, (CHUNK, CHUNK), 0)
    si = lax.broadcasted_iota(jnp.int32, (CHUNK, CHUNK), 1)
    causal = (li <= si) if reverse else (li >= si)
    tri = jnp.where(causal, 1.0, 0.0).astype(jnp.bfloat16)
    tri_t = jnp.where((li >= si) if reverse else (li <= si), 1.0, 0.0).astype(jnp.bfloat16)
    cs = sum(jnp.dot(tri, p, preferred_element_type=jnp.float32) for p in _split3(adt))
    cs_t = sum(jnp.dot(p, tri_t, preferred_element_type=jnp.float32) for p in _split3(adt_t))
    edge = 0 if reverse else CHUNK - 1
    total = cs[edge:edge + 1, :]
    total_t = cs_t[:, edge:edge + 1]
    hp = lax.broadcasted_iota(jnp.int32, (SSD_HEADS, D_INNER), 0)
    hc = lax.broadcasted_iota(jnp.int32, (SSD_HEADS, D_INNER), 1) // SSD_HEAD_DIM
    expand = jnp.where(hp == hc, 1.0, 0.0).astype(jnp.bfloat16)
    xdt = xs_ref[...] * _dot2(dt, expand)
    xdte = (xdt * _dot2(jnp.exp(total - cs), expand)).astype(jnp.bfloat16)
    xdt = xdt.astype(jnp.bfloat16)
    ecs = _dot2(jnp.exp(cs), expand)
    chunk_decay = jnp.exp(total_t)

    for g in range(SSD_GROUPS):
        b_t = bt_ref[g]
        c_g = c_ref[:, g * SSD_STATE:(g + 1) * SSD_STATE]
        cb = jnp.dot(c_g, b_t, preferred_element_type=jnp.float32)
        for hl in range(HEADS_PER_GROUP):
            hh = g * HEADS_PER_GROUP + hl
            ps = slice(hh * SSD_HEAD_DIM, (hh + 1) * SSD_HEAD_DIM)
            seg = cs[:, hh:hh + 1] - cs_t[hh:hh + 1, :]
            w = jnp.where(causal, cb * jnp.exp(jnp.where(causal, seg, 0.0)), 0.0).astype(jnp.bfloat16)
            state = state_ref[hh]
            y = jnp.dot(w, xdt[:, ps], preferred_element_type=jnp.float32)
            y = y + jnp.dot(c_g, state.astype(jnp.bfloat16), preferred_element_type=jnp.float32) * ecs[:, ps]
            y_ref[:, ps] = y
            state_ref[hh] = state * chunk_decay[hh:hh + 1, :] + jnp.dot(
                b_t, xdte[:, ps], preferred_element_type=jnp.float32)

    if not reverse:
        o_ref[...] = y_ref[...].astype(o_ref.dtype)
        return
    y = y_ref[...] + yf_ref[...].astype(jnp.float32) + dskip_ref[...] * xs_ref[...]
    gated = y * _silu(z_ref[...].astype(jnp.float32))
    normed = _rms(gated, gn_ref[...]).astype(jnp.bfloat16)
    out = jnp.dot(normed, wout_ref[...], preferred_element_type=jnp.float32)
    rowd = lax.broadcasted_iota(jnp.int32, out.shape, 0)
    keep = jnp.logical_or(jnp.logical_not(seq_first), rowd >= META_PAD)
    o_ref[...] = jnp.where(keep, h_ref[...] + out, 0.0)


def _chunk_flags(seq_rows):
    first, last = [], []
    for n in seq_rows:
        assert n % CHUNK == 0
        c = n // CHUNK
        first += [1] + [0] * (c - 1)
        last += [0] * (c - 1) + [1]
    return jnp.asarray(first, jnp.int32), jnp.asarray(last, jnp.int32)


def ssd_scan(seq_rows, xbc, dt, dtt, conv_w, conv_b, dt_bias, a_log, reverse, final=None):
    rows = xbc.shape[0]
    n_chunks = rows // CHUNK
    first, last = _chunk_flags(seq_rows)
    d = 1 if reverse else 0
    halo_per_chunk = CHUNK // HALO_ROWS
    n_halo = rows // HALO_ROWS
    cid = (lambda i, *_: n_chunks - 1 - i) if reverse else (lambda i, *_: i)
    rowblk = lambda i, *_: (cid(i), 0)
    const2 = lambda i, *_: (0, 0)
    once = lambda shape: pl.BlockSpec(shape, const2, pipeline_mode=pl.Buffered(1))
    in_specs = [
        pl.BlockSpec((HALO_ROWS, CONV_DIM), lambda i, *_: (jnp.maximum(cid(i) * halo_per_chunk - 1, 0), 0)),
        pl.BlockSpec((CHUNK, CONV_DIM), rowblk),
        pl.BlockSpec((HALO_ROWS, CONV_DIM),
                     lambda i, *_: (jnp.minimum((cid(i) + 1) * halo_per_chunk, n_halo - 1), 0)),
        pl.BlockSpec((CHUNK, 2 * SSD_HEADS), rowblk),
        pl.BlockSpec((SSD_HEADS, CHUNK), lambda i, *_: (d, cid(i))),
        once((CONV_W, CONV_DIM)), once((1, CONV_DIM)),
        once((1, SSD_HEADS)), once((SSD_HEADS, 1)), once((1, SSD_HEADS)), once((SSD_HEADS, 1)),
    ]
    args = [xbc, xbc, xbc, dt, dtt, conv_w, conv_b[None],
            dt_bias[d][None], dt_bias[d][:, None], a_log[d][None], a_log[d][:, None]]
    resident = 0
    if reverse:
        yf, z, h, d_skip, gate_g, w_out = final
        wob = w_out.astype(jnp.bfloat16)
        in_specs += [
            pl.BlockSpec((CHUNK, D_INNER), rowblk), pl.BlockSpec((CHUNK, D_INNER), rowblk),
            pl.BlockSpec((CHUNK, D_MODEL), rowblk),
            once((1, D_INNER)), once((1, D_INNER)), once(wob.shape),
        ]
        args += [yf, z, h, jnp.repeat(d_skip, SSD_HEAD_DIM)[None], gate_g[None], wob]
        out_spec = pl.BlockSpec((CHUNK, D_MODEL), rowblk)
        out_shape = jax.ShapeDtypeStruct((rows, D_MODEL), jnp.float32)
        resident = wob.size * 2
    else:
        out_spec = pl.BlockSpec((CHUNK, D_INNER), rowblk)
        out_shape = jax.ShapeDtypeStruct((rows, D_INNER), jnp.float32)
    return pl.pallas_call(
        functools.partial(_ssd_scan_kernel, reverse=reverse),
        grid_spec=pltpu.PrefetchScalarGridSpec(
            num_scalar_prefetch=2,
            grid=(n_chunks,),
            in_specs=in_specs,
            out_specs=out_spec,
            scratch_shapes=[
                pltpu.VMEM((CHUNK + 2 * SUBLANES, CONV_DIM), jnp.float32),
                pltpu.VMEM((CHUNK, D_INNER), jnp.float32),
                pltpu.VMEM((SSD_GROUPS, SSD_STATE, CHUNK), jnp.bfloat16),
                pltpu.VMEM((CHUNK, SSD_GROUPS * SSD_STATE), jnp.bfloat16),
                pltpu.VMEM((CHUNK, D_INNER), jnp.float32),
                pltpu.VMEM((SSD_HEADS, SSD_STATE, SSD_HEAD_DIM), jnp.float32),
            ],
        ),
        out_shape=out_shape,
        compiler_params=pltpu.CompilerParams(
            dimension_semantics=("arbitrary",), vmem_limit_bytes=_vmem_limit(resident)),
        name="ssd_scan_bwd" if reverse else "ssd_scan_fwd",
    )(first, last, *args)


def ssd_layer(seq_rows, h, g, w_in, conv_w, conv_b, dt_bias, a_log, d_skip, gate_g, w_out):
    z, xbc, dt, dtt = ssd_in(h, g, w_in)
    yf = ssd_scan(seq_rows, xbc, dt, dtt, conv_w, conv_b, dt_bias, a_log, reverse=False)
    return ssd_scan(seq_rows, xbc, dt, dtt, conv_w, conv_b, dt_bias, a_log, reverse=True,
                    final=(yf, z, h, d_skip, gate_g, w_out))


ATTN_HEAD_DIM = 128
N_Q_HEADS = 8
N_KV_HEADS = 2
KV_REP = N_Q_HEADS // N_KV_HEADS
Q_WIDTH = N_Q_HEADS * ATTN_HEAD_DIM
KV_WIDTH = N_KV_HEADS * ATTN_HEAD_DIM
ATTN_SCALE = ATTN_HEAD_DIM ** -0.5
GRID_W = 64
ROPE_THETA = 10000.0
ROPE_FREQS = ATTN_HEAD_DIM // 4
KEY_BLOCK = 128
Q_ROWS = 128


def rope_tables(seq_rows_max):
    r = np.arange(seq_rows_max)
    tok = r - FRONT
    meta = r - META_PAD
    grow = np.where(tok >= 0, tok // GRID_W, -1).astype(np.float32)
    gcol = np.where(tok >= 0, tok % GRID_W, np.maximum(meta, 0)).astype(np.float32)
    inv_freq = jnp.asarray(ROPE_THETA, jnp.float32) ** (-jnp.arange(ROPE_FREQS, dtype=jnp.float32) / ROPE_FREQS)
    ang = jnp.concatenate([jnp.asarray(grow)[:, None] * inv_freq, jnp.asarray(gcol)[:, None] * inv_freq], axis=-1)
    ang = jnp.concatenate([ang, ang], axis=-1)
    sign = np.where(np.arange(ATTN_HEAD_DIM) < ATTN_HEAD_DIM // 2, -1.0, 1.0).astype(np.float32)
    return jnp.cos(ang), jnp.sin(ang) * sign


def _attn_in_kernel(h_ref, g_ref, wq_ref, wkt_ref, wv_ref, qg_ref, kg_ref, cos_ref, sin_ref, cost_ref, sint_ref,
                    q_ref, kt_ref, v_ref):
    rows = h_ref.shape[0]
    xb = _rms(h_ref[...], g_ref[...]).astype(jnp.bfloat16)
    q = jnp.dot(xb, wq_ref[...], preferred_element_type=jnp.float32)
    v_ref[...] = jnp.dot(xb, wv_ref[...], preferred_element_type=jnp.float32).astype(v_ref.dtype)
    kt = lax.dot_general(wkt_ref[...], xb, (((1,), (1,)), ((), ())), preferred_element_type=jnp.float32)
    cos, sin = cos_ref[...], sin_ref[...]
    half = ATTN_HEAD_DIM // 2
    for hd in range(N_Q_HEADS):
        sl = slice(hd * ATTN_HEAD_DIM, (hd + 1) * ATTN_HEAD_DIM)
        x = _rms(q[:, sl], qg_ref[...])
        rot = jnp.concatenate([x[:, half:], x[:, :half]], axis=-1)
        q_ref[:, sl] = ((x * cos + rot * sin) * ATTN_SCALE).astype(q_ref.dtype)
    cos_t, sin_t = cost_ref[...], sint_ref[...]
    for hd in range(N_KV_HEADS):
        x = kt[hd * ATTN_HEAD_DIM:(hd + 1) * ATTN_HEAD_DIM, :]
        x = x * lax.rsqrt(jnp.mean(x * x, axis=0, keepdims=True) + RMS_EPS) * kg_ref[...]
        rot = jnp.concatenate([x[half:, :], x[:half, :]], axis=0)
        x = (x * cos_t + rot * sin_t).astype(kt_ref.dtype)
        for b in range(rows // KEY_BLOCK):
            kt_ref[b, hd * ATTN_HEAD_DIM:(hd + 1) * ATTN_HEAD_DIM, :] = x[:, b * KEY_BLOCK:(b + 1) * KEY_BLOCK]


def attn_in(h, g, w_qkv, q_g, k_g, cos, sin):
    rows = h.shape[0]
    tm = DENSE_ROWS
    assert rows % tm == 0
    wb = w_qkv.astype(jnp.bfloat16)
    wq, wk, wv = wb[:, :Q_WIDTH], wb[:, Q_WIDTH:Q_WIDTH + KV_WIDTH], wb[:, Q_WIDTH + KV_WIDTH:]
    once = lambda shape: pl.BlockSpec(shape, lambda i: (0,) * len(shape), pipeline_mode=pl.Buffered(1))
    return pl.pallas_call(
        _attn_in_kernel,
        grid=(rows // tm,),
        in_specs=[
            pl.BlockSpec((tm, D_MODEL), lambda i: (i, 0)),
            once((1, D_MODEL)), once(wq.shape), once((KV_WIDTH, D_MODEL)), once(wv.shape),
            once((1, ATTN_HEAD_DIM)), once((ATTN_HEAD_DIM, 1)),
            pl.BlockSpec((tm, ATTN_HEAD_DIM), lambda i: (i, 0)),
            pl.BlockSpec((tm, ATTN_HEAD_DIM), lambda i: (i, 0)),
            pl.BlockSpec((ATTN_HEAD_DIM, tm), lambda i: (0, i)),
            pl.BlockSpec((ATTN_HEAD_DIM, tm), lambda i: (0, i)),
        ],
        out_specs=[
            pl.BlockSpec((tm, Q_WIDTH), lambda i: (i, 0)),
            pl.BlockSpec((tm // KEY_BLOCK, KV_WIDTH, KEY_BLOCK), lambda i: (i, 0, 0)),
            pl.BlockSpec((tm, KV_WIDTH), lambda i: (i, 0)),
        ],
        out_shape=[
            jax.ShapeDtypeStruct((rows, Q_WIDTH), jnp.bfloat16),
            jax.ShapeDtypeStruct((rows // KEY_BLOCK, KV_WIDTH, KEY_BLOCK), jnp.bfloat16),
            jax.ShapeDtypeStruct((rows, KV_WIDTH), jnp.bfloat16),
        ],
        compiler_params=pltpu.CompilerParams(
            dimension_semantics=("arbitrary",), vmem_limit_bytes=_vmem_limit(wb.size * 2)),
        name="attn_in",
    )(h, g, wq, wk.T, wv, q_g[None], k_g[:, None], cos, sin, cos.T, sin.T)


def _attn_kernel(q_ref, kt_ref, v_ref, h_ref, wo_ref, o_ref, o_acc, *, key_blocks):
    n_blocks = kt_ref.shape[0]
    n_steps = n_blocks // key_blocks
    width = key_blocks * KEY_BLOCK
    lane = lax.broadcasted_iota(jnp.int32, (1, width), 1)
    for g in range(N_KV_HEADS):
        ks = slice(g * ATTN_HEAD_DIM, (g + 1) * ATTN_HEAD_DIM)
        q = jnp.concatenate([q_ref[:, (g * KV_REP + r) * ATTN_HEAD_DIM:(g * KV_REP + r + 1) * ATTN_HEAD_DIM]
                             for r in range(KV_REP)], axis=0)

        def scores(step):
            kt = jnp.concatenate([kt_ref[step * key_blocks + b, ks, :] for b in range(key_blocks)], axis=1)
            return jnp.dot(q, kt, preferred_element_type=jnp.float32)

        def update(carry, s, step):
            m, l, acc = carry
            m_new = jnp.maximum(m, jnp.max(s, axis=-1, keepdims=True))
            alpha = jnp.exp(m - m_new)
            p = jnp.exp(s - m_new)
            v = v_ref[pl.ds(pl.multiple_of(step * width, width), width), ks]
            acc = alpha * acc + jnp.dot(p.astype(jnp.bfloat16), v, preferred_element_type=jnp.float32)
            return m_new, alpha * l + jnp.sum(p, axis=-1, keepdims=True), acc

        s0 = jnp.where(lane >= META_PAD, scores(0), NEG_INF)
        init = (jnp.full((KV_REP * Q_ROWS, 1), NEG_INF, jnp.float32),
                jnp.zeros((KV_REP * Q_ROWS, 1), jnp.float32),
                jnp.zeros((KV_REP * Q_ROWS, ATTN_HEAD_DIM), jnp.float32))
        carry = update(init, s0, 0)
        m, l, acc = lax.fori_loop(1, n_steps, lambda st, c: update(c, scores(st), st), carry)
        out = (acc / l).astype(jnp.bfloat16)
        for r in range(KV_REP):
            hd = g * KV_REP + r
            o_acc[:, hd * ATTN_HEAD_DIM:(hd + 1) * ATTN_HEAD_DIM] = out[r * Q_ROWS:(r + 1) * Q_ROWS]
    o_ref[...] = h_ref[...] + jnp.dot(o_acc[...], wo_ref[...], preferred_element_type=jnp.float32)


def _key_blocks_per_step(n_blocks):
    return max(d for d in range(1, 6) if n_blocks % d == 0)


def attention(n_seq, q, kt, v, h, w_o):
    rows = q.shape[0]
    seq = rows // n_seq
    n_blocks = seq // KEY_BLOCK
    qb = seq // Q_ROWS
    wob = w_o.astype(jnp.bfloat16)
    kernel_fn = functools.partial(_attn_kernel, key_blocks=_key_blocks_per_step(n_blocks))
    return pl.pallas_call(
        kernel_fn,
        grid=(n_seq, qb),
        in_specs=[
            pl.BlockSpec((Q_ROWS, Q_WIDTH), lambda b, i: (b * qb + i, 0)),
            pl.BlockSpec((n_blocks, KV_WIDTH, KEY_BLOCK), lambda b, i: (b, 0, 0), pipeline_mode=pl.Buffered(1)),
            pl.BlockSpec((seq, KV_WIDTH), lambda b, i: (b, 0), pipeline_mode=pl.Buffered(1)),
            pl.BlockSpec((Q_ROWS, D_MODEL), lambda b, i: (b * qb + i, 0)),
            pl.BlockSpec(wob.shape, lambda b, i: (0, 0), pipeline_mode=pl.Buffered(1)),
        ],
        out_specs=pl.BlockSpec((Q_ROWS, D_MODEL), lambda b, i: (b * qb + i, 0)),
        out_shape=jax.ShapeDtypeStruct((rows, D_MODEL), jnp.float32),
        scratch_shapes=[pltpu.VMEM((Q_ROWS, Q_WIDTH), jnp.bfloat16)],
        compiler_params=pltpu.CompilerParams(
            dimension_semantics=("arbitrary", "arbitrary"),
            vmem_limit_bytes=_vmem_limit(2 * seq * KV_WIDTH * 2 + wob.size * 2)),
        name="attention",
    )(q, kt, v, h, wob)


def _to_rows(x, meta):
    b, s, d = x.shape
    front = jnp.concatenate([jnp.zeros((META_PAD, d), x.dtype), meta.astype(x.dtype)], axis=0)
    rows = jnp.concatenate([jnp.broadcast_to(front[None], (b, FRONT, d)), x], axis=1)
    return rows.reshape(b * (FRONT + s), d)


def kernel(x_prompt, x_sample, meta_tokens, norm_mix_g, norm_ffn_g, ssd_w_in, ssd_conv_w, ssd_conv_b, ssd_dt_bias,
           ssd_a_log, ssd_d_skip, ssd_gate_norm_g, ssd_w_out, attn_w_qkv, attn_q_norm_g, attn_k_norm_g, attn_w_o,
           peer_w_q, peer_sub_keys, peer_u, peer_v):
    trunks = (x_prompt, x_sample)
    seq_rows = [FRONT + x.shape[1] for x in trunks for _ in range(x.shape[0])]
    trunk_rows = [x.shape[0] * (FRONT + x.shape[1]) for x in trunks]
    h = jnp.concatenate([_to_rows(x, meta_tokens) for x in trunks], axis=0)

    cos, sin = rope_tables(max(seq_rows))
    cos_all = jnp.concatenate([cos[:n] for n in seq_rows], axis=0)
    sin_all = jnp.concatenate([sin[:n] for n in seq_rows], axis=0)

    depth = norm_mix_g.shape[0]
    for i in range(depth):
        j = i // 2
        g_mix = norm_mix_g[i][None]
        if i % 2 == 0:
            h = ssd_layer(seq_rows, h, g_mix, ssd_w_in[j], ssd_conv_w[j], ssd_conv_b[j], ssd_dt_bias[j],
                          ssd_a_log[j], ssd_d_skip[j], ssd_gate_norm_g[j], ssd_w_out[j])
        else:
            q, kt, v = attn_in(h, g_mix, attn_w_qkv[j], attn_q_norm_g[j], attn_k_norm_g[j], cos_all, sin_all)
            parts, r0 = [], 0
            for x, n in zip(trunks, trunk_rows):
                r1 = r0 + n
                parts.append(attention(x.shape[0], q[r0:r1], kt[r0 // KEY_BLOCK:r1 // KEY_BLOCK], v[r0:r1],
                                       h[r0:r1], attn_w_o[j]))
                r0 = r1
            h = jnp.concatenate(parts, axis=0)
        wqt = peer_w_q[i].T.astype(jnp.bfloat16)
        keys = peer_sub_keys[i].reshape(2 * PEER_HEADS, N_KEYS, PEER_HALF).astype(jnp.bfloat16)
        h = peer_layer(h, norm_ffn_g[i][None], wqt, keys,
                       pack_expert_table(peer_u[i]), pack_expert_table(peer_v[i]))

    outs, r0 = [], 0
    for x, n in zip(trunks, trunk_rows):
        b, s, d = x.shape
        outs.append(h[r0:r0 + n].reshape(b, FRONT + s, d)[:, FRONT:])
        r0 += n
    return tuple(outs)
```

```python
import functools

import jax
import jax.numpy as jnp
import numpy as np
from jax import lax
from jax.experimental import pallas as pl
from jax.experimental.pallas import tpu as pltpu

LANES = 128
SUBLANES = 8
VREG_ELEMS = LANES * SUBLANES
V7X_VMEM_BYTES = 64 * 1024 * 1024

D_MODEL = 1024
N_META = 16
FRONT = 128
META_PAD = FRONT - N_META
RMS_EPS = 1e-6

PEER_HEADS = 8
N_KEYS = 128
PEER_HALF = 128
PEER_TOPK = 16
N_ASSIGN = PEER_HEADS * PEER_TOPK
N_EXPERTS = N_KEYS * N_KEYS
D_CHUNKS = D_MODEL // LANES
HALF_CHUNKS = D_CHUNKS // 2
PAIR_ROW_OF_CHUNK = tuple(2 * (c % HALF_CHUNKS) + c // HALF_CHUNKS for c in range(D_CHUNKS))

NEG_INF = float("-inf")


def _vmem_limit(resident_bytes):
    return int(min(V7X_VMEM_BYTES - 8 * 1024 * 1024, resident_bytes + 20 * 1024 * 1024))


def _rms(x, g):
    return x * lax.rsqrt(jnp.mean(x * x, axis=-1, keepdims=True) + RMS_EPS) * g


PEER_Q_ROWS = 256
PAIRS = tuple((i, j) for i in range(PEER_TOPK) for j in range(PEER_TOPK) if (i + 1) * (j + 1) <= PEER_TOPK)
N_PAIRS = len(PAIRS)
PAIR_ROWS = -(-N_PAIRS // SUBLANES) * SUBLANES


def _top16_rows(s, n_rows):
    rid = lax.broadcasted_iota(jnp.int32, s.shape, 0)
    vals, ids = [], []
    for _ in range(PEER_TOPK):
        m = jnp.max(s, axis=0, keepdims=True)
        i = jnp.min(jnp.where(s == m, rid, n_rows), axis=0, keepdims=True)
        vals.append(m)
        ids.append(i)
        s = jnp.where(rid == i, NEG_INF, s)
    return jnp.concatenate(vals, axis=0), jnp.concatenate(ids, axis=0)


def _pair_selectors():
    sel0 = np.zeros((PAIR_ROWS, PEER_TOPK), np.float32)
    sel1 = np.zeros((PAIR_ROWS, PEER_TOPK), np.float32)
    for r, (i, j) in enumerate(PAIRS):
        sel0[r, i] = 1.0
        sel1[r, j] = 1.0
    return jnp.asarray(sel0, jnp.bfloat16), jnp.asarray(sel1, jnp.bfloat16)


def _peer_topk_kernel(h_ref, g_ref, wqt_ref, keys_ref, sel0_ref, sel1_ref, x3_ref, eid_ref, gate_ref):
    rows = h_ref.shape[0]
    x = _rms(h_ref[...], g_ref[...])
    for c in range(D_CHUNKS):
        x3_ref[pl.ds(PAIR_ROW_OF_CHUNK[c], rows, stride=D_CHUNKS), :] = x[:, c * LANES:(c + 1) * LANES]
    xb = x.astype(jnp.bfloat16)
    eids, gates = [], []
    for hd in range(PEER_HEADS):
        sv, si = [], []
        for c in range(2):
            j = hd * 2 + c
            qt = lax.dot_general(wqt_ref[j * PEER_HALF:(j + 1) * PEER_HALF, :], xb,
                                 (((1,), (1,)), ((), ())), preferred_element_type=jnp.float32)
            st = jnp.dot(keys_ref[j], qt.astype(jnp.bfloat16), preferred_element_type=jnp.float32)
            v, i = _top16_rows(st, N_KEYS)
            sv.append(v)
            si.append(i)
        cand = _dot3(sel0_ref[...], sv[0], left=True) + _dot3(sel1_ref[...], sv[1], left=True)
        cidx = (jnp.dot(sel0_ref[...], si[0].astype(jnp.bfloat16), preferred_element_type=jnp.float32) * N_KEYS
                + jnp.dot(sel1_ref[...], si[1].astype(jnp.bfloat16), preferred_element_type=jnp.float32)
                ).astype(jnp.int32)
        pos = lax.broadcasted_iota(jnp.int32, cand.shape, 0)
        cand = jnp.where(pos < N_PAIRS, cand, NEG_INF)
        tops, sel = [], []
        for _ in range(PEER_TOPK):
            m = jnp.max(cand, axis=0, keepdims=True)
            p = jnp.min(jnp.where(cand == m, pos, PAIR_ROWS), axis=0, keepdims=True)
            hit = pos == p
            tops.append(m)
            sel.append(jnp.sum(jnp.where(hit, cidx, 0), axis=0, keepdims=True))
            cand = jnp.where(hit, NEG_INF, cand)
        top = jnp.concatenate(tops, axis=0)
        e = jnp.exp(top - top[0:1])
        gates.append(e / jnp.sum(e, axis=0, keepdims=True))
        eids.append(jnp.concatenate(sel, axis=0))
    gate_t = jnp.concatenate(gates, axis=0)
    eid_t = jnp.concatenate(eids, axis=0)
    for b in range(rows // LANES):
        sl = slice(b * LANES, (b + 1) * LANES)
        gate_ref[sl, :] = gate_t[:, sl].T
        eid_ref[sl, :] = eid_t[:, sl].T * HALF_CHUNKS


def peer_topk(h, g, wqt, keys):
    rows = h.shape[0]
    tb = PEER_Q_ROWS
    assert rows % tb == 0
    const2 = lambda i: (0, 0)
    sel0, sel1 = _pair_selectors()
    return pl.pallas_call(
        _peer_topk_kernel,
        grid=(rows // tb,),
        in_specs=[
            pl.BlockSpec((tb, D_MODEL), lambda i: (i, 0)),
            pl.BlockSpec((1, D_MODEL), const2),
            pl.BlockSpec(wqt.shape, const2, pipeline_mode=pl.Buffered(1)),
            pl.BlockSpec(keys.shape, lambda i: (0, 0, 0), pipeline_mode=pl.Buffered(1)),
            pl.BlockSpec(sel0.shape, const2),
            pl.BlockSpec(sel1.shape, const2),
        ],
        out_specs=[
            pl.BlockSpec((tb * D_CHUNKS, LANES), lambda i: (i, 0)),
            pl.BlockSpec((tb, N_ASSIGN), lambda i: (i, 0)),
            pl.BlockSpec((tb, N_ASSIGN), lambda i: (i, 0)),
        ],
        out_shape=[
            jax.ShapeDtypeStruct((rows * D_CHUNKS, LANES), jnp.float32),
            jax.ShapeDtypeStruct((rows, N_ASSIGN), jnp.int32),
            jax.ShapeDtypeStruct((rows, N_ASSIGN), jnp.float32),
        ],
        compiler_params=pltpu.CompilerParams(
            dimension_semantics=("arbitrary",),
            vmem_limit_bytes=_vmem_limit(wqt.size * 2 + keys.size * 2)),
        name="peer_topk",
    )(h, g, wqt, keys, sel0, sel1)


PEER_ROWS = 128
PEER_GROUP = 4


def _tile_scratch():
    return [pltpu.VMEM((N_ASSIGN * HALF_CHUNKS, LANES), jnp.uint32) for _ in range(2 * PEER_GROUP)]


def pack_expert_table(w):
    b = lax.bitcast_convert_type(w.astype(jnp.bfloat16), jnp.uint16).astype(jnp.uint32)
    b = b.reshape(N_EXPERTS, 2, HALF_CHUNKS, LANES)
    return (b[:, 0] | (b[:, 1] << 16)).reshape(N_EXPERTS * HALF_CHUNKS, LANES)


def _gather_tile(eid_ref, t, tbl_ref, tile_ref):
    for a in range(N_ASSIGN):
        e4 = pl.multiple_of(eid_ref[t, a], HALF_CHUNKS)
        tile_ref[a * HALF_CHUNKS:(a + 1) * HALF_CHUNKS, :] = tbl_ref[pl.ds(e4, HALF_CHUNKS), :]


def _split3(x):
    hi = x.astype(jnp.bfloat16)
    r = x - hi.astype(jnp.float32)
    mid = r.astype(jnp.bfloat16)
    lo = (r - mid.astype(jnp.float32)).astype(jnp.bfloat16)
    return hi, mid, lo


def _dot3(x, m, left=False):
    if left:
        return sum(jnp.dot(x, p, preferred_element_type=jnp.float32) for p in _split3(m))
    return sum(jnp.dot(p, m, preferred_element_type=jnp.float32) for p in _split3(x))


def _diag_mask(shape):
    q = lax.broadcasted_iota(jnp.int32, shape, 0)
    n = lax.broadcasted_iota(jnp.int32, shape, 1)
    return (n % SUBLANES) == q


def _for_each_token(rows, gather, compute, tiles_a, tiles_b):
    group = len(tiles_a)
    n_groups = rows // group

    def gather_group(j, tiles):
        for k, tile in enumerate(tiles):
            gather(j * group + k, tile)

    def compute_group(j, tiles):
        for k, tile in enumerate(tiles):
            compute(j * group + k, tile)

    gather_group(0, tiles_a)

    def pair(i, carry):
        j = 2 * i
        compute_group(j, tiles_a)
        gather_group(j + 1, tiles_b)

        @pl.when(j + 1 < n_groups)
        def _():
            compute_group(j + 1, tiles_b)
            gather_group(jnp.minimum(j + 2, n_groups - 1), tiles_a)

        return carry

    lax.fori_loop(0, n_groups // 2, pair, 0)


def _peer_act_kernel(eid_ref, x3_ref, gate_ref, group_ref, tbl_ref, w_ref, part_ref, *tiles):
    rows = gate_ref.shape[0]
    mask = _diag_mask((SUBLANES, N_ASSIGN * SUBLANES))
    zeros = jnp.zeros((SUBLANES, LANES), jnp.bfloat16)

    def compute(t, tile_ref):
        xt = x3_ref[pl.ds(pl.multiple_of(t * SUBLANES, SUBLANES), SUBLANES), :].astype(jnp.bfloat16)
        lhs = jnp.concatenate([xt, zeros], axis=0)
        u = pltpu.bitcast(tile_ref[...], jnp.bfloat16)
        r = lax.dot_general(lhs, u, (((1,), (1,)), ((), ())), preferred_element_type=jnp.float32)
        part_ref[pl.ds(t, 1), :] = jnp.sum(jnp.where(mask, r[0:SUBLANES], 0.0), axis=0, keepdims=True)

    gather = lambda t, tile_ref: _gather_tile(eid_ref, t, tbl_ref, tile_ref)
    _for_each_token(rows, gather, compute, tiles[:PEER_GROUP], tiles[PEER_GROUP:])
    act = _dot3(part_ref[...], group_ref[...])
    gelu = 0.5 * act * (1.0 + lax.erf(act * np.float32(1.0 / np.sqrt(2.0))))
    w_ref[...] = gate_ref[...] * gelu


def _peer_out_kernel(eid_ref, w_ref, h_ref, expand_ref, tbl_ref, o_ref, wexp_ref, acc_ref, *tiles):
    rows = w_ref.shape[0]
    mask = _diag_mask((SUBLANES, N_ASSIGN * SUBLANES))
    wexp_ref[...] = _dot3(w_ref[...], expand_ref[...])

    def compute(t, tile_ref):
        wrow = jnp.where(mask, wexp_ref[pl.ds(t, 1), :], 0.0)
        hi = wrow.astype(jnp.bfloat16)
        lo = (wrow - hi.astype(jnp.float32)).astype(jnp.bfloat16)
        lhs = jnp.concatenate([hi, lo], axis=0)
        v = pltpu.bitcast(tile_ref[...], jnp.bfloat16)
        r = jnp.dot(lhs, v, preferred_element_type=jnp.float32)
        acc_ref[pl.ds(pl.multiple_of(t * SUBLANES, SUBLANES), SUBLANES), :] = r[0:SUBLANES] + r[SUBLANES:]

    gather = lambda t, tile_ref: _gather_tile(eid_ref, t, tbl_ref, tile_ref)
    _for_each_token(rows, gather, compute, tiles[:PEER_GROUP], tiles[PEER_GROUP:])
    for c in range(D_CHUNKS):
        sl = slice(c * LANES, (c + 1) * LANES)
        o_ref[:, sl] = h_ref[:, sl] + acc_ref[pl.ds(PAIR_ROW_OF_CHUNK[c], rows, stride=D_CHUNKS), :]


def _group_matrix():
    n = np.arange(N_ASSIGN * SUBLANES)
    return jnp.asarray((n[:, None] // SUBLANES) == np.arange(N_ASSIGN)[None, :], jnp.bfloat16)


def peer_act(eid, x3, gate, tbl):
    rows = eid.shape[0]
    tb = PEER_ROWS
    assert rows % tb == 0
    group = _group_matrix()
    return pl.pallas_call(
        _peer_act_kernel,
        grid=(rows // tb,),
        in_specs=[
            pl.BlockSpec((tb, N_ASSIGN), lambda i: (i, 0), memory_space=pltpu.SMEM),
            pl.BlockSpec((tb * D_CHUNKS, LANES), lambda i: (i, 0)),
            pl.BlockSpec((tb, N_ASSIGN), lambda i: (i, 0)),
            pl.BlockSpec(group.shape, lambda i: (0, 0)),
            pl.BlockSpec(tbl.shape, lambda i: (0, 0), pipeline_mode=pl.Buffered(1)),
        ],
        out_specs=pl.BlockSpec((tb, N_ASSIGN), lambda i: (i, 0)),
        out_shape=jax.ShapeDtypeStruct((rows, N_ASSIGN), jnp.float32),
        scratch_shapes=[pltpu.VMEM((tb, N_ASSIGN * SUBLANES), jnp.float32)] + _tile_scratch(),
        compiler_params=pltpu.CompilerParams(
            dimension_semantics=("arbitrary",), vmem_limit_bytes=_vmem_limit(tbl.size * 4)),
        name="peer_act",
    )(eid, x3, gate, group, tbl)


def peer_out(eid, w, h, tbl):
    rows = eid.shape[0]
    tb = PEER_ROWS
    assert rows % tb == 0
    expand = _group_matrix().T
    return pl.pallas_call(
        _peer_out_kernel,
        grid=(rows // tb,),
        in_specs=[
            pl.BlockSpec((tb, N_ASSIGN), lambda i: (i, 0), memory_space=pltpu.SMEM),
            pl.BlockSpec((tb, N_ASSIGN), lambda i: (i, 0)),
            pl.BlockSpec((tb, D_MODEL), lambda i: (i, 0)),
            pl.BlockSpec(expand.shape, lambda i: (0, 0)),
            pl.BlockSpec(tbl.shape, lambda i: (0, 0), pipeline_mode=pl.Buffered(1)),
        ],
        out_specs=pl.BlockSpec((tb, D_MODEL), lambda i: (i, 0)),
        out_shape=jax.ShapeDtypeStruct((rows, D_MODEL), jnp.float32),
        scratch_shapes=[
            pltpu.VMEM((tb, N_ASSIGN * SUBLANES), jnp.float32),
            pltpu.VMEM((tb * D_CHUNKS, LANES), jnp.float32),
        ] + _tile_scratch(),
        compiler_params=pltpu.CompilerParams(
            dimension_semantics=("arbitrary",), vmem_limit_bytes=_vmem_limit(tbl.size * 4)),
        name="peer_out",
    )(eid, w, h, expand, tbl)


def peer_layer(h, g, wqt, keys, u_tbl, v_tbl):
    x3, eid, gate = peer_topk(h, g, wqt, keys)
    w = peer_act(eid, x3, gate, u_tbl)
    return peer_out(eid, w, h, v_tbl)


D_INNER = 2048
SSD_HEAD_DIM = 64
SSD_HEADS = D_INNER // SSD_HEAD_DIM
SSD_GROUPS = 4
HEADS_PER_GROUP = SSD_HEADS // SSD_GROUPS
SSD_STATE = 128
CONV_W = 5
CONV_HALF = (CONV_W - 1) // 2
CONV_DIM = D_INNER + 2 * SSD_GROUPS * SSD_STATE
CHUNK = 128
HALO_ROWS = 16
DENSE_ROWS = 256


def _ssd_in_kernel(h_ref, g_ref, wz_ref, wx_ref, wdt_ref, wdtt_ref, z_ref, xbc_ref, dt_ref, dtt_ref):
    xb = _rms(h_ref[...], g_ref[...]).astype(jnp.bfloat16)
    z_ref[...] = jnp.dot(xb, wz_ref[...], preferred_element_type=jnp.float32).astype(z_ref.dtype)
    xbc_ref[...] = jnp.dot(xb, wx_ref[...], preferred_element_type=jnp.float32).astype(xbc_ref.dtype)
    dt_ref[...] = jnp.dot(xb, wdt_ref[...], preferred_element_type=jnp.float32)
    dtt_ref[...] = lax.dot_general(wdtt_ref[...], xb, (((1,), (1,)), ((), ())),
                                   preferred_element_type=jnp.float32)


def ssd_in(h, g, w_in):
    rows = h.shape[0]
    tm = DENSE_ROWS
    assert rows % tm == 0
    wb = w_in.astype(jnp.bfloat16)
    wz, wx, wdt = wb[:, :D_INNER], wb[:, D_INNER:D_INNER + CONV_DIM], wb[:, D_INNER + CONV_DIM:]
    n_dt = wdt.shape[1]
    once = lambda shape: pl.BlockSpec(shape, lambda i: (0,) * len(shape), pipeline_mode=pl.Buffered(1))
    return pl.pallas_call(
        _ssd_in_kernel,
        grid=(rows // tm,),
        in_specs=[
            pl.BlockSpec((tm, D_MODEL), lambda i: (i, 0)),
            once((1, D_MODEL)), once(wz.shape), once(wx.shape), once(wdt.shape), once((n_dt, D_MODEL)),
        ],
        out_specs=[
            pl.BlockSpec((tm, D_INNER), lambda i: (i, 0)),
            pl.BlockSpec((tm, CONV_DIM), lambda i: (i, 0)),
            pl.BlockSpec((tm, n_dt), lambda i: (i, 0)),
            pl.BlockSpec((n_dt, tm), lambda i: (0, i)),
        ],
        out_shape=[
            jax.ShapeDtypeStruct((rows, D_INNER), jnp.bfloat16),
            jax.ShapeDtypeStruct((rows, CONV_DIM), jnp.bfloat16),
            jax.ShapeDtypeStruct((rows, n_dt), jnp.float32),
            jax.ShapeDtypeStruct((n_dt, rows), jnp.float32),
        ],
        compiler_params=pltpu.CompilerParams(
            dimension_semantics=("arbitrary",), vmem_limit_bytes=_vmem_limit(wb.size * 2)),
        name="ssd_in",
    )(h, g, wz, wx, wdt, wdt.T)


def _dot2(x, m):
    hi = x.astype(jnp.bfloat16)
    lo = (x - hi.astype(jnp.float32)).astype(jnp.bfloat16)
    return (jnp.dot(hi, m, preferred_element_type=jnp.float32)
            + jnp.dot(lo, m, preferred_element_type=jnp.float32))


def _softplus(x):
    return jnp.maximum(x, 0.0) + jnp.log(1.0 + jnp.exp(-jnp.abs(x)))


def _silu(x):
    return x / (1.0 + jnp.exp(-x))


def _ssd_scan_kernel(first_ref, last_ref,
                     prev_ref, cur_ref, next_ref, dt_ref, dtt_ref,
                     convw_ref, convb_ref, bias_ref, biast_ref, alog_ref, alogt_ref, *rest,
                     reverse):
    if reverse:
        (yf_ref, z_ref, h_ref, dskip_ref, gn_ref, wout_ref, o_ref,
         ext_ref, xs_ref, bt_ref, c_ref, y_ref, state_ref) = rest
    else:
        o_ref, ext_ref, xs_ref, bt_ref, c_ref, y_ref, state_ref = rest
    ci = pl.program_id(0)
    n_chunks = pl.num_programs(0)
    chunk_id = (n_chunks - 1 - ci) if reverse else ci
    seq_first = first_ref[chunk_id] == 1
    seq_last = last_ref[chunk_id] == 1
    starts = seq_last if reverse else seq_first

    @pl.when(starts)
    def _():
        state_ref[...] = jnp.zeros_like(state_ref)

    keep_prev = jnp.where(seq_first, 0.0, 1.0)
    keep_next = jnp.where(seq_last, 0.0, 1.0)
    ext_ref[0:SUBLANES, :] = prev_ref[HALO_ROWS - SUBLANES:, :].astype(jnp.float32) * keep_prev
    ext_ref[SUBLANES:SUBLANES + CHUNK, :] = cur_ref[...].astype(jnp.float32)
    ext_ref[SUBLANES + CHUNK:, :] = next_ref[0:SUBLANES, :].astype(jnp.float32) * keep_next
    strip = 4 * LANES
    for s0 in range(0, CONV_DIM, strip):
        acc = jnp.zeros((CHUNK, strip), jnp.float32) + convb_ref[:, s0:s0 + strip]
        for k in range(CONV_W):
            r0 = SUBLANES - CONV_HALF + k
            acc = acc + ext_ref[r0:r0 + CHUNK, s0:s0 + strip] * convw_ref[k:k + 1, s0:s0 + strip]
        act = _silu(acc)
        if s0 < D_INNER:
            xs_ref[:, s0:s0 + strip] = act
        elif s0 < D_INNER + SSD_GROUPS * SSD_STATE:
            for g in range(strip // SSD_STATE):
                gi = (s0 - D_INNER) // SSD_STATE + g
                bt_ref[gi] = act[:, g * SSD_STATE:(g + 1) * SSD_STATE].T.astype(jnp.bfloat16)
        else:
            b0 = s0 - D_INNER - SSD_GROUPS * SSD_STATE
            c_ref[:, b0:b0 + strip] = act.astype(jnp.bfloat16)

    d0 = SSD_HEADS if reverse else 0
    row = lax.broadcasted_iota(jnp.int32, (CHUNK, SSD_HEADS), 0)
    col = lax.broadcasted_iota(jnp.int32, (SSD_HEADS, CHUNK), 1)
    valid = jnp.logical_or(jnp.logical_not(seq_first), row >= META_PAD)
    valid_t = jnp.logical_or(jnp.logical_not(seq_first), col >= META_PAD)
    dt = jnp.where(valid, _softplus(dt_ref[:, d0:d0 + SSD_HEADS] + bias_ref[...]), 0.0)
    dt_t = jnp.where(valid_t, _softplus(dtt_ref[...] + biast_ref[...]), 0.0)
    adt = dt * -jnp.exp(alog_ref[...])
    adt_t = dt_t * -jnp.exp(alogt_ref[...])
    li = lax.broadcasted_iota(jnp.int32, (CHUNK, CHUNK), 0)
    si = lax.broadcasted_iota(jnp.int32, (CHUNK, CHUNK), 1)
    causal = (li <= si) if reverse else (li >= si)
    tri = jnp.where(causal, 1.0, 0.0).astype(jnp.bfloat16)
    tri_t = jnp.where((li >= si) if reverse else (li <= si), 1.0, 0.0).astype(jnp.bfloat16)
    cs = sum(jnp.dot(tri, p, preferred_element_type=jnp.float32) for p in _split3(adt))
    cs_t = sum(jnp.dot(p, tri_t, preferred_element_type=jnp.float32) for p in _split3(adt_t))
    edge = 0 if reverse else CHUNK - 1
    total = cs[edge:edge + 1, :]
    total_t = cs_t[:, edge:edge + 1]
    hp = lax.broadcasted_iota(jnp.int32, (SSD_HEADS, D_INNER), 0)
    hc = lax.broadcasted_iota(jnp.int32, (SSD_HEADS, D_INNER), 1) // SSD_HEAD_DIM
    expand = jnp.where(hp == hc, 1.0, 0.0).astype(jnp.bfloat16)
    xdt = xs_ref[...] * _dot2(dt, expand)
    xdte = (xdt * _dot2(jnp.exp(total - cs), expand)).astype(jnp.bfloat16)
    xdt = xdt.astype(jnp.bfloat16)
    ecs = _dot2(jnp.exp(cs), expand)
    chunk_decay = jnp.exp(total_t)

    for g in range(SSD_GROUPS):
        b_t = bt_ref[g]
        c_g = c_ref[:, g * SSD_STATE:(g + 1) * SSD_STATE]
        cb = jnp.dot(c_g, b_t, preferred_element_type=jnp.float32)
        for hl in range(HEADS_PER_GROUP):
            hh = g * HEADS_PER_GROUP + hl
            ps = slice(hh * SSD_HEAD_DIM, (hh + 1) * SSD_HEAD_DIM)
            seg = cs[:, hh:hh + 1] - cs_t[hh:hh + 1, :]
            w = jnp.where(causal, cb * jnp.exp(jnp.where(causal, seg, 0.0)), 0.0).astype(jnp.bfloat16)
            state = state_ref[hh]
            y = jnp.dot(w, xdt[:, ps], preferred_element_type=jnp.float32)
            y = y + jnp.dot(c_g, state.astype(jnp.bfloat16), preferred_element_type=jnp.float32) * ecs[:, ps]
            y_ref[:, ps] = y
            state_ref[hh] = state * chunk_decay[hh:hh + 1, :] + jnp.dot(
                b_t, xdte[:, ps], preferred_element_type=jnp.float32)

    if not reverse:
        o_ref[...] = y_ref[...].astype(o_ref.dtype)
        return
    y = y_ref[...] + yf_ref[...].astype(jnp.float32) + dskip_ref[...] * xs_ref[...]
    gated = y * _silu(z_ref[...].astype(jnp.float32))
    normed = _rms(gated, gn_ref[...]).astype(jnp.bfloat16)
    out = jnp.dot(normed, wout_ref[...], preferred_element_type=jnp.float32)
    rowd = lax.broadcasted_iota(jnp.int32, out.shape, 0)
    keep = jnp.logical_or(jnp.logical_not(seq_first), rowd >= META_PAD)
    o_ref[...] = jnp.where(keep, h_ref[...] + out, 0.0)


def _chunk_flags(seq_rows):
    first, last = [], []
    for n in seq_rows:
        assert n % CHUNK == 0
        c = n // CHUNK
        first += [1] + [0] * (c - 1)
        last += [0] * (c - 1) + [1]
    return jnp.asarray(first, jnp.int32), jnp.asarray(last, jnp.int32)


def ssd_scan(seq_rows, xbc, dt, dtt, conv_w, conv_b, dt_bias, a_log, reverse, final=None):
    rows = xbc.shape[0]
    n_chunks = rows // CHUNK
    first, last = _chunk_flags(seq_rows)
    d = 1 if reverse else 0
    halo_per_chunk = CHUNK // HALO_ROWS
    n_halo = rows // HALO_ROWS
    cid = (lambda i, *_: n_chunks - 1 - i) if reverse else (lambda i, *_: i)
    rowblk = lambda i, *_: (cid(i), 0)
    const2 = lambda i, *_: (0, 0)
    once = lambda shape: pl.BlockSpec(shape, const2, pipeline_mode=pl.Buffered(1))
    in_specs = [
        pl.BlockSpec((HALO_ROWS, CONV_DIM), lambda i, *_: (jnp.maximum(cid(i) * halo_per_chunk - 1, 0), 0)),
        pl.BlockSpec((CHUNK, CONV_DIM), rowblk),
        pl.BlockSpec((HALO_ROWS, CONV_DIM),
                     lambda i, *_: (jnp.minimum((cid(i) + 1) * halo_per_chunk, n_halo - 1), 0)),
        pl.BlockSpec((CHUNK, 2 * SSD_HEADS), rowblk),
        pl.BlockSpec((SSD_HEADS, CHUNK), lambda i, *_: (d, cid(i))),
        once((CONV_W, CONV_DIM)), once((1, CONV_DIM)),
        once((1, SSD_HEADS)), once((SSD_HEADS, 1)), once((1, SSD_HEADS)), once((SSD_HEADS, 1)),
    ]
    args = [xbc, xbc, xbc, dt, dtt, conv_w, conv_b[None],
            dt_bias[d][None], dt_bias[d][:, None], a_log[d][None], a_log[d][:, None]]
    resident = 0
    if reverse:
        yf, z, h, d_skip, gate_g, w_out = final
        wob = w_out.astype(jnp.bfloat16)
        in_specs += [
            pl.BlockSpec((CHUNK, D_INNER), rowblk), pl.BlockSpec((CHUNK, D_INNER), rowblk),
            pl.BlockSpec((CHUNK, D_MODEL), rowblk),
            once((1, D_INNER)), once((1, D_INNER)), once(wob.shape),
        ]
        args += [yf, z, h, jnp.repeat(d_skip, SSD_HEAD_DIM)[None], gate_g[None], wob]
        out_spec = pl.BlockSpec((CHUNK, D_MODEL), rowblk)
        out_shape = jax.ShapeDtypeStruct((rows, D_MODEL), jnp.float32)
        resident = wob.size * 2
    else:
        out_spec = pl.BlockSpec((CHUNK, D_INNER), rowblk)
        out_shape = jax.ShapeDtypeStruct((rows, D_INNER), jnp.float32)
    return pl.pallas_call(
        functools.partial(_ssd_scan_kernel, reverse=reverse),
        grid_spec=pltpu.PrefetchScalarGridSpec(
            num_scalar_prefetch=2,
            grid=(n_chunks,),
            in_specs=in_specs,
            out_specs=out_spec,
            scratch_shapes=[
                pltpu.VMEM((CHUNK + 2 * SUBLANES, CONV_DIM), jnp.float32),
                pltpu.VMEM((CHUNK, D_INNER), jnp.float32),
                pltpu.VMEM((SSD_GROUPS, SSD_STATE, CHUNK), jnp.bfloat16),
                pltpu.VMEM((CHUNK, SSD_GROUPS * SSD_STATE), jnp.bfloat16),
                pltpu.VMEM((CHUNK, D_INNER), jnp.float32),
                pltpu.VMEM((SSD_HEADS, SSD_STATE, SSD_HEAD_DIM), jnp.float32),
            ],
        ),
        out_shape=out_shape,
        compiler_params=pltpu.CompilerParams(
            dimension_semantics=("arbitrary",), vmem_limit_bytes=_vmem_limit(resident)),
        name="ssd_scan_bwd" if reverse else "ssd_scan_fwd",
    )(first, last, *args)


def ssd_layer(seq_rows, h, g, w_in, conv_w, conv_b, dt_bias, a_log, d_skip, gate_g, w_out):
    z, xbc, dt, dtt = ssd_in(h, g, w_in)
    yf = ssd_scan(seq_rows, xbc, dt, dtt, conv_w, conv_b, dt_bias, a_log, reverse=False)
    return ssd_scan(seq_rows, xbc, dt, dtt, conv_w, conv_b, dt_bias, a_log, reverse=True,
                    final=(yf, z, h, d_skip, gate_g, w_out))


ATTN_HEAD_DIM = 128
N_Q_HEADS = 8
N_KV_HEADS = 2
KV_REP = N_Q_HEADS // N_KV_HEADS
Q_WIDTH = N_Q_HEADS * ATTN_HEAD_DIM
KV_WIDTH = N_KV_HEADS * ATTN_HEAD_DIM
ATTN_SCALE = ATTN_HEAD_DIM ** -0.5
Q_SCALE_LOG2 = float(ATTN_SCALE * np.log2(np.e))
GRID_W = 64
ROPE_THETA = 10000.0
ROPE_FREQS = ATTN_HEAD_DIM // 4
KEY_BLOCK = 128
Q_ROWS = 128


def rope_tables(seq_rows_max):
    r = np.arange(seq_rows_max)
    tok = r - FRONT
    meta = r - META_PAD
    grow = np.where(tok >= 0, tok // GRID_W, -1).astype(np.float32)
    gcol = np.where(tok >= 0, tok % GRID_W, np.maximum(meta, 0)).astype(np.float32)
    inv_freq = jnp.asarray(ROPE_THETA, jnp.float32) ** (-jnp.arange(ROPE_FREQS, dtype=jnp.float32) / ROPE_FREQS)
    ang = jnp.concatenate([jnp.asarray(grow)[:, None] * inv_freq, jnp.asarray(gcol)[:, None] * inv_freq], axis=-1)
    ang = jnp.concatenate([ang, ang], axis=-1)
    sign = np.where(np.arange(ATTN_HEAD_DIM) < ATTN_HEAD_DIM // 2, -1.0, 1.0).astype(np.float32)
    return jnp.cos(ang), jnp.sin(ang) * sign


def _attn_in_kernel(h_ref, g_ref, wq_ref, wkt_ref, wv_ref, qg_ref, kg_ref, cos_ref, sin_ref, cost_ref, sint_ref,
                    q_ref, kt_ref, v_ref):
    rows = h_ref.shape[0]
    xb = _rms(h_ref[...], g_ref[...]).astype(jnp.bfloat16)
    q = jnp.dot(xb, wq_ref[...], preferred_element_type=jnp.float32)
    v_ref[...] = jnp.dot(xb, wv_ref[...], preferred_element_type=jnp.float32).astype(v_ref.dtype)
    kt = lax.dot_general(wkt_ref[...], xb, (((1,), (1,)), ((), ())), preferred_element_type=jnp.float32)
    cos, sin = cos_ref[...], sin_ref[...]
    half = ATTN_HEAD_DIM // 2
    for hd in range(N_Q_HEADS):
        sl = slice(hd * ATTN_HEAD_DIM, (hd + 1) * ATTN_HEAD_DIM)
        x = _rms(q[:, sl], qg_ref[...])
        rot = jnp.concatenate([x[:, half:], x[:, :half]], axis=-1)
        q_ref[:, sl] = ((x * cos + rot * sin) * Q_SCALE_LOG2).astype(q_ref.dtype)
    cos_t, sin_t = cost_ref[...], sint_ref[...]
    for hd in range(N_KV_HEADS):
        x = kt[hd * ATTN_HEAD_DIM:(hd + 1) * ATTN_HEAD_DIM, :]
        x = x * lax.rsqrt(jnp.mean(x * x, axis=0, keepdims=True) + RMS_EPS) * kg_ref[...]
        rot = jnp.concatenate([x[half:, :], x[:half, :]], axis=0)
        x = (x * cos_t + rot * sin_t).astype(kt_ref.dtype)
        for b in range(rows // KEY_BLOCK):
            kt_ref[b, hd * ATTN_HEAD_DIM:(hd + 1) * ATTN_HEAD_DIM, :] = x[:, b * KEY_BLOCK:(b + 1) * KEY_BLOCK]


def attn_in(h, g, w_qkv, q_g, k_g, cos, sin):
    rows = h.shape[0]
    tm = DENSE_ROWS
    assert rows % tm == 0
    wb = w_qkv.astype(jnp.bfloat16)
    wq, wk, wv = wb[:, :Q_WIDTH], wb[:, Q_WIDTH:Q_WIDTH + KV_WIDTH], wb[:, Q_WIDTH + KV_WIDTH:]
    once = lambda shape: pl.BlockSpec(shape, lambda i: (0,) * len(shape), pipeline_mode=pl.Buffered(1))
    return pl.pallas_call(
        _attn_in_kernel,
        grid=(rows // tm,),
        in_specs=[
            pl.BlockSpec((tm, D_MODEL), lambda i: (i, 0)),
            once((1, D_MODEL)), once(wq.shape), once((KV_WIDTH, D_MODEL)), once(wv.shape),
            once((1, ATTN_HEAD_DIM)), once((ATTN_HEAD_DIM, 1)),
            pl.BlockSpec((tm, ATTN_HEAD_DIM), lambda i: (i, 0)),
            pl.BlockSpec((tm, ATTN_HEAD_DIM), lambda i: (i, 0)),
            pl.BlockSpec((ATTN_HEAD_DIM, tm), lambda i: (0, i)),
            pl.BlockSpec((ATTN_HEAD_DIM, tm), lambda i: (0, i)),
        ],
        out_specs=[
            pl.BlockSpec((tm, Q_WIDTH), lambda i: (i, 0)),
            pl.BlockSpec((tm // KEY_BLOCK, KV_WIDTH, KEY_BLOCK), lambda i: (i, 0, 0)),
            pl.BlockSpec((tm, KV_WIDTH), lambda i: (i, 0)),
        ],
        out_shape=[
            jax.ShapeDtypeStruct((rows, Q_WIDTH), jnp.bfloat16),
            jax.ShapeDtypeStruct((rows // KEY_BLOCK, KV_WIDTH, KEY_BLOCK), jnp.bfloat16),
            jax.ShapeDtypeStruct((rows, KV_WIDTH), jnp.bfloat16),
        ],
        compiler_params=pltpu.CompilerParams(
            dimension_semantics=("arbitrary",), vmem_limit_bytes=_vmem_limit(wb.size * 2)),
        name="attn_in",
    )(h, g, wq, wk.T, wv, q_g[None], k_g[:, None], cos, sin, cos.T, sin.T)


def _attn_kernel(q_ref, kt_ref, v_ref, h_ref, wo_ref, o_ref, o_acc, *, key_blocks):
    n_blocks = kt_ref.shape[0]
    n_steps = (n_blocks - 1) // key_blocks
    stacked = KV_REP * Q_ROWS
    qs = [jnp.concatenate([q_ref[:, (g * KV_REP + r) * ATTN_HEAD_DIM:(g * KV_REP + r + 1) * ATTN_HEAD_DIM]
                           for r in range(KV_REP)], axis=0) for g in range(N_KV_HEADS)]

    def step(block0, blocks, carry, mask_front):
        width = blocks * KEY_BLOCK
        ones_col = jnp.where(lax.broadcasted_iota(jnp.int32, (width, LANES), 1) == 0, 1.0, 0.0
                             ).astype(jnp.bfloat16)
        new = []
        for g in range(N_KV_HEADS):
            m, l, acc = carry[g]
            ks = slice(g * ATTN_HEAD_DIM, (g + 1) * ATTN_HEAD_DIM)
            kt = jnp.concatenate([kt_ref[block0 + b, ks, :] for b in range(blocks)], axis=1)
            s = jnp.dot(qs[g], kt, preferred_element_type=jnp.float32)
            if mask_front:
                s = jnp.where(lax.broadcasted_iota(jnp.int32, (1, width), 1) >= META_PAD, s, NEG_INF)
            m_new = jnp.maximum(m, jnp.max(s, axis=-1, keepdims=True))
            alpha = jnp.exp2(m - m_new)
            p = jnp.exp2(s - m_new).astype(jnp.bfloat16)
            v = v_ref[pl.ds(pl.multiple_of(block0 * KEY_BLOCK, KEY_BLOCK), width), ks]
            r = jnp.dot(p, jnp.concatenate([v, ones_col], axis=1), preferred_element_type=jnp.float32)
            new.append((m_new, alpha * l + r[:, ATTN_HEAD_DIM:ATTN_HEAD_DIM + 1],
                        alpha * acc + r[:, :ATTN_HEAD_DIM]))
        return tuple(new)

    init = tuple((jnp.full((stacked, 1), NEG_INF, jnp.float32), jnp.zeros((stacked, 1), jnp.float32),
                  jnp.zeros((stacked, ATTN_HEAD_DIM), jnp.float32)) for _ in range(N_KV_HEADS))
    carry = step(0, 1, init, True)
    carry = lax.fori_loop(0, n_steps, lambda st, c: step(1 + st * key_blocks, key_blocks, c, False), carry)
    for g in range(N_KV_HEADS):
        _, l, acc = carry[g]
        out = (acc / l).astype(jnp.bfloat16)
        for r in range(KV_REP):
            hd = g * KV_REP + r
            o_acc[:, hd * ATTN_HEAD_DIM:(hd + 1) * ATTN_HEAD_DIM] = out[r * Q_ROWS:(r + 1) * Q_ROWS]
    o_ref[...] = h_ref[...] + jnp.dot(o_acc[...], wo_ref[...], preferred_element_type=jnp.float32)


MAX_KEY_BLOCKS = 8


def _key_blocks_per_step(n_blocks):
    return max(d for d in range(1, MAX_KEY_BLOCKS + 1) if (n_blocks - 1) % d == 0)


def attention(n_seq, q, kt, v, h, w_o):
    rows = q.shape[0]
    seq = rows // n_seq
    n_blocks = seq // KEY_BLOCK
    qb = seq // Q_ROWS
    wob = w_o.astype(jnp.bfloat16)
    kernel_fn = functools.partial(_attn_kernel, key_blocks=_key_blocks_per_step(n_blocks))
    return pl.pallas_call(
        kernel_fn,
        grid=(n_seq, qb),
        in_specs=[
            pl.BlockSpec((Q_ROWS, Q_WIDTH), lambda b, i: (b * qb + i, 0)),
            pl.BlockSpec((n_blocks, KV_WIDTH, KEY_BLOCK), lambda b, i: (b, 0, 0), pipeline_mode=pl.Buffered(1)),
            pl.BlockSpec((seq, KV_WIDTH), lambda b, i: (b, 0), pipeline_mode=pl.Buffered(1)),
            pl.BlockSpec((Q_ROWS, D_MODEL), lambda b, i: (b * qb + i, 0)),
            pl.BlockSpec(wob.shape, lambda b, i: (0, 0), pipeline_mode=pl.Buffered(1)),
        ],
        out_specs=pl.BlockSpec((Q_ROWS, D_MODEL), lambda b, i: (b * qb + i, 0)),
        out_shape=jax.ShapeDtypeStruct((rows, D_MODEL), jnp.float32),
        scratch_shapes=[pltpu.VMEM((Q_ROWS, Q_WIDTH), jnp.bfloat16)],
        compiler_params=pltpu.CompilerParams(
            dimension_semantics=("arbitrary", "arbitrary"),
            vmem_limit_bytes=_vmem_limit(2 * seq * KV_WIDTH * 2 + wob.size * 2)),
        name="attention",
    )(q, kt, v, h, wob)


def _to_rows(x, meta):
    b, s, d = x.shape
    front = jnp.concatenate([jnp.zeros((META_PAD, d), x.dtype), meta.astype(x.dtype)], axis=0)
    rows = jnp.concatenate([jnp.broadcast_to(front[None], (b, FRONT, d)), x], axis=1)
    return rows.reshape(b * (FRONT + s), d)


def kernel(x_prompt, x_sample, meta_tokens, norm_mix_g, norm_ffn_g, ssd_w_in, ssd_conv_w, ssd_conv_b, ssd_dt_bias,
           ssd_a_log, ssd_d_skip, ssd_gate_norm_g, ssd_w_out, attn_w_qkv, attn_q_norm_g, attn_k_norm_g, attn_w_o,
           peer_w_q, peer_sub_keys, peer_u, peer_v):
    trunks = (x_prompt, x_sample)
    seq_rows = [FRONT + x.shape[1] for x in trunks for _ in range(x.shape[0])]
    trunk_rows = [x.shape[0] * (FRONT + x.shape[1]) for x in trunks]
    h = jnp.concatenate([_to_rows(x, meta_tokens) for x in trunks], axis=0)

    cos, sin = rope_tables(max(seq_rows))
    cos_all = jnp.concatenate([cos[:n] for n in seq_rows], axis=0)
    sin_all = jnp.concatenate([sin[:n] for n in seq_rows], axis=0)

    depth = norm_mix_g.shape[0]
    for i in range(depth):
        j = i // 2
        g_mix = norm_mix_g[i][None]
        if i % 2 == 0:
            h = ssd_layer(seq_rows, h, g_mix, ssd_w_in[j], ssd_conv_w[j], ssd_conv_b[j], ssd_dt_bias[j],
                          ssd_a_log[j], ssd_d_skip[j], ssd_gate_norm_g[j], ssd_w_out[j])
        else:
            q, kt, v = attn_in(h, g_mix, attn_w_qkv[j], attn_q_norm_g[j], attn_k_norm_g[j], cos_all, sin_all)
            parts, r0 = [], 0
            for x, n in zip(trunks, trunk_rows):
                r1 = r0 + n
                parts.append(attention(x.shape[0], q[r0:r1], kt[r0 // KEY_BLOCK:r1 // KEY_BLOCK], v[r0:r1],
                                       h[r0:r1], attn_w_o[j]))
                r0 = r1
            h = jnp.concatenate(parts, axis=0)
        wqt = peer_w_q[i].T.astype(jnp.bfloat16)
        keys = peer_sub_keys[i].reshape(2 * PEER_HEADS, N_KEYS, PEER_HALF).astype(jnp.bfloat16)
        h = peer_layer(h, norm_ffn_g[i][None], wqt, keys,
                       pack_expert_table(peer_u[i]), pack_expert_table(peer_v[i]))

    outs, r0 = [], 0
    for x, n in zip(trunks, trunk_rows):
        b, s, d = x.shape
        outs.append(h[r0:r0 + n].reshape(b, FRONT + s, d)[:, FRONT:])
        r0 += n
    return tuple(outs)
```

```python
import functools

import jax
import jax.numpy as jnp
import numpy as np
from jax import lax
from jax.experimental import pallas as pl
from jax.experimental.pallas import tpu as pltpu

LANES = 128
SUBLANES = 8
VREG_ELEMS = LANES * SUBLANES
V7X_VMEM_BYTES = 64 * 1024 * 1024

D_MODEL = 1024
N_META = 16
FRONT = 128
META_PAD = FRONT - N_META
RMS_EPS = 1e-6

PEER_HEADS = 8
N_KEYS = 128
PEER_HALF = 128
PEER_TOPK = 16
N_ASSIGN = PEER_HEADS * PEER_TOPK
N_EXPERTS = N_KEYS * N_KEYS
D_CHUNKS = D_MODEL // LANES
HALF_CHUNKS = D_CHUNKS // 2

NEG_INF = float("-inf")


def _vmem_limit(resident_bytes):
    return int(min(V7X_VMEM_BYTES - 8 * 1024 * 1024, resident_bytes + 20 * 1024 * 1024))


def _rms(x, g):
    return x * lax.rsqrt(jnp.mean(x * x, axis=-1, keepdims=True) + RMS_EPS) * g


PEER_Q_ROWS = 256
PAIRS = tuple((i, j) for i in range(PEER_TOPK) for j in range(PEER_TOPK) if (i + 1) * (j + 1) <= PEER_TOPK)
N_PAIRS = len(PAIRS)
PAIR_ROWS = -(-N_PAIRS // SUBLANES) * SUBLANES


def _top16_rows(s, n_rows):
    rid = lax.broadcasted_iota(jnp.int32, s.shape, 0)
    vals, ids = [], []
    for _ in range(PEER_TOPK):
        m = jnp.max(s, axis=0, keepdims=True)
        i = jnp.min(jnp.where(s == m, rid, n_rows), axis=0, keepdims=True)
        vals.append(m)
        ids.append(i)
        s = jnp.where(rid == i, NEG_INF, s)
    return jnp.concatenate(vals, axis=0), jnp.concatenate(ids, axis=0)


def _pair_selectors():
    sel0 = np.zeros((PAIR_ROWS, PEER_TOPK), np.float32)
    sel1 = np.zeros((PAIR_ROWS, PEER_TOPK), np.float32)
    for r, (i, j) in enumerate(PAIRS):
        sel0[r, i] = 1.0
        sel1[r, j] = 1.0
    return jnp.asarray(sel0, jnp.bfloat16), jnp.asarray(sel1, jnp.bfloat16)


def _peer_topk_kernel(h_ref, g_ref, wqt_ref, keys_ref, sel0_ref, sel1_ref, x3_ref, eid_ref, gate_ref):
    rows = h_ref.shape[0]
    x = _rms(h_ref[...], g_ref[...])
    for c in range(D_CHUNKS):
        x3_ref[pl.ds(c, rows, stride=D_CHUNKS), :] = x[:, c * LANES:(c + 1) * LANES]
    xb = x.astype(jnp.bfloat16)
    eids, gates = [], []
    for hd in range(PEER_HEADS):
        sv, si = [], []
        for c in range(2):
            j = hd * 2 + c
            qt = lax.dot_general(wqt_ref[j * PEER_HALF:(j + 1) * PEER_HALF, :], xb,
                                 (((1,), (1,)), ((), ())), preferred_element_type=jnp.float32)
            st = jnp.dot(keys_ref[j], qt.astype(jnp.bfloat16), preferred_element_type=jnp.float32)
            v, i = _top16_rows(st, N_KEYS)
            sv.append(v)
            si.append(i)
        cand = _dot3(sel0_ref[...], sv[0], left=True) + _dot3(sel1_ref[...], sv[1], left=True)
        cidx = (jnp.dot(sel0_ref[...], si[0].astype(jnp.bfloat16), preferred_element_type=jnp.float32) * N_KEYS
                + jnp.dot(sel1_ref[...], si[1].astype(jnp.bfloat16), preferred_element_type=jnp.float32)
                ).astype(jnp.int32)
        pos = lax.broadcasted_iota(jnp.int32, cand.shape, 0)
        cand = jnp.where(pos < N_PAIRS, cand, NEG_INF)
        tops, sel = [], []
        for _ in range(PEER_TOPK):
            m = jnp.max(cand, axis=0, keepdims=True)
            p = jnp.min(jnp.where(cand == m, pos, PAIR_ROWS), axis=0, keepdims=True)
            hit = pos == p
            tops.append(m)
            sel.append(jnp.sum(jnp.where(hit, cidx, 0), axis=0, keepdims=True))
            cand = jnp.where(hit, NEG_INF, cand)
        top = jnp.concatenate(tops, axis=0)
        e = jnp.exp(top - top[0:1])
        gates.append(e / jnp.sum(e, axis=0, keepdims=True))
        eids.append(jnp.concatenate(sel, axis=0))
    gate_t = jnp.concatenate(gates, axis=0)
    eid_t = jnp.concatenate(eids, axis=0)
    for b in range(rows // LANES):
        sl = slice(b * LANES, (b + 1) * LANES)
        gate_ref[sl, :] = gate_t[:, sl].T
        eid_ref[sl, :] = eid_t[:, sl].T * HALF_CHUNKS


def peer_topk(h, g, wqt, keys):
    rows = h.shape[0]
    tb = PEER_Q_ROWS
    assert rows % tb == 0
    const2 = lambda i: (0, 0)
    sel0, sel1 = _pair_selectors()
    return pl.pallas_call(
        _peer_topk_kernel,
        grid=(rows // tb,),
        in_specs=[
            pl.BlockSpec((tb, D_MODEL), lambda i: (i, 0)),
            pl.BlockSpec((1, D_MODEL), const2),
            pl.BlockSpec(wqt.shape, const2, pipeline_mode=pl.Buffered(1)),
            pl.BlockSpec(keys.shape, lambda i: (0, 0, 0), pipeline_mode=pl.Buffered(1)),
            pl.BlockSpec(sel0.shape, const2),
            pl.BlockSpec(sel1.shape, const2),
        ],
        out_specs=[
            pl.BlockSpec((tb * D_CHUNKS, LANES), lambda i: (i, 0)),
            pl.BlockSpec((tb, N_ASSIGN), lambda i: (i, 0)),
            pl.BlockSpec((tb, N_ASSIGN), lambda i: (i, 0)),
        ],
        out_shape=[
            jax.ShapeDtypeStruct((rows * D_CHUNKS, LANES), jnp.float32),
            jax.ShapeDtypeStruct((rows, N_ASSIGN), jnp.int32),
            jax.ShapeDtypeStruct((rows, N_ASSIGN), jnp.float32),
        ],
        compiler_params=pltpu.CompilerParams(
            dimension_semantics=("arbitrary",),
            vmem_limit_bytes=_vmem_limit(wqt.size * 2 + keys.size * 2)),
        name="peer_topk",
    )(h, g, wqt, keys, sel0, sel1)


PEER_ROWS = 128
HIGH_HALF = 0xFFFF0000


def _unpack_pair(words):
    return (lax.bitcast_convert_type(words << 16, jnp.float32),
            lax.bitcast_convert_type(words & jnp.uint32(HIGH_HALF), jnp.float32))


def pack_expert_table(w):
    b = lax.bitcast_convert_type(w.astype(jnp.bfloat16), jnp.uint16).astype(jnp.uint32)
    b = b.reshape(N_EXPERTS, 2, HALF_CHUNKS, LANES)
    return (b[:, 0] | (b[:, 1] << 16)).reshape(N_EXPERTS * HALF_CHUNKS, LANES)


def _split3(x):
    hi = x.astype(jnp.bfloat16)
    r = x - hi.astype(jnp.float32)
    mid = r.astype(jnp.bfloat16)
    lo = (r - mid.astype(jnp.float32)).astype(jnp.bfloat16)
    return hi, mid, lo


def _dot3(x, m, left=False):
    if left:
        return sum(jnp.dot(x, p, preferred_element_type=jnp.float32) for p in _split3(m))
    return sum(jnp.dot(p, m, preferred_element_type=jnp.float32) for p in _split3(x))


ACT_GROUP = 4
PAIR_CHUNK_OF_ROW = tuple((q % 2) * HALF_CHUNKS + q // 2 for q in range(D_CHUNKS))


def _gather_tile(eid_ref, t, tbl_ref, tile_ref):
    for a in range(N_ASSIGN):
        e4 = pl.multiple_of(eid_ref[t, a], HALF_CHUNKS)
        tile_ref[a * HALF_CHUNKS:(a + 1) * HALF_CHUNKS, :] = tbl_ref[pl.ds(e4, HALF_CHUNKS), :]


def _for_each_token(rows, gather, compute, tiles_a, tiles_b):
    group = len(tiles_a)
    n_groups = rows // group

    def gather_group(j, tiles):
        for k, tile in enumerate(tiles):
            gather(j * group + k, tile)

    def compute_group(j, tiles):
        for k, tile in enumerate(tiles):
            compute(j * group + k, tile)

    gather_group(0, tiles_a)

    def pair(i, carry):
        j = 2 * i
        compute_group(j, tiles_a)
        gather_group(j + 1, tiles_b)

        @pl.when(j + 1 < n_groups)
        def _():
            compute_group(j + 1, tiles_b)
            gather_group(jnp.minimum(j + 2, n_groups - 1), tiles_a)

        return carry

    lax.fori_loop(0, n_groups // 2, pair, 0)


def _peer_act_kernel(eid_ref, x3_ref, gate_ref, group_ref, tbl_ref, w_ref, part_ref, *tiles):
    rows = gate_ref.shape[0]
    q = lax.broadcasted_iota(jnp.int32, (SUBLANES, N_ASSIGN * SUBLANES), 0)
    n = lax.broadcasted_iota(jnp.int32, (SUBLANES, N_ASSIGN * SUBLANES), 1)
    mask = (n % SUBLANES) == q
    zeros = jnp.zeros((SUBLANES, LANES), jnp.bfloat16)

    def compute(t, tile_ref):
        xt = x3_ref[pl.ds(pl.multiple_of(t * D_CHUNKS, D_CHUNKS), D_CHUNKS), :]
        xp = jnp.concatenate([xt[c:c + 1] for c in PAIR_CHUNK_OF_ROW], axis=0).astype(jnp.bfloat16)
        lhs = jnp.concatenate([xp, zeros], axis=0)
        u = pltpu.bitcast(tile_ref[...], jnp.bfloat16)
        r = lax.dot_general(lhs, u, (((1,), (1,)), ((), ())), preferred_element_type=jnp.float32)
        part_ref[pl.ds(t, 1), :] = jnp.sum(jnp.where(mask, r[0:SUBLANES], 0.0), axis=0, keepdims=True)

    gather = lambda t, tile_ref: _gather_tile(eid_ref, t, tbl_ref, tile_ref)
    _for_each_token(rows, gather, compute, tiles[:ACT_GROUP], tiles[ACT_GROUP:])
    act = _dot3(part_ref[...], group_ref[...])
    gelu = 0.5 * act * (1.0 + lax.erf(act * np.float32(1.0 / np.sqrt(2.0))))
    w_ref[...] = gate_ref[...] * gelu


def _peer_out_kernel(eid_ref, w_ref, h_ref, tbl_ref, o_ref, wrep_ref, acc_ref):
    rows = w_ref.shape[0]

    def expand(t):
        return jnp.broadcast_to(w_ref[pl.ds(t, 1), :], (LANES, N_ASSIGN)).T

    def token(t, wrep):
        wrep_ref[...] = wrep
        wrep_next = expand(jnp.minimum(t + 1, rows - 1))
        lo = [jnp.zeros((HALF_CHUNKS, LANES), jnp.float32) for _ in range(2)]
        hi = [jnp.zeros((HALF_CHUNKS, LANES), jnp.float32) for _ in range(2)]
        for a in range(N_ASSIGN):
            e4 = pl.multiple_of(eid_ref[t, a], HALF_CHUNKS)
            row_lo, row_hi = _unpack_pair(tbl_ref[pl.ds(e4, HALF_CHUNKS), :])
            wv = wrep_ref[pl.ds(a, 1), :]
            lo[a % 2] = lo[a % 2] + wv * row_lo
            hi[a % 2] = hi[a % 2] + wv * row_hi
        base = pl.multiple_of(t * D_CHUNKS, D_CHUNKS)
        acc_ref[pl.ds(base, HALF_CHUNKS), :] = lo[0] + lo[1]
        acc_ref[pl.ds(base + HALF_CHUNKS, HALF_CHUNKS), :] = hi[0] + hi[1]
        return wrep_next

    lax.fori_loop(0, rows, token, expand(0))
    for c in range(D_CHUNKS):
        sl = slice(c * LANES, (c + 1) * LANES)
        o_ref[:, sl] = h_ref[:, sl] + acc_ref[pl.ds(c, rows, stride=D_CHUNKS), :]


def _group_matrix():
    n = np.arange(N_ASSIGN * SUBLANES)
    return jnp.asarray((n[:, None] // SUBLANES) == np.arange(N_ASSIGN)[None, :], jnp.bfloat16)


def peer_act(eid, x3, gate, tbl):
    rows = eid.shape[0]
    tb = PEER_ROWS
    assert rows % tb == 0
    group = _group_matrix()
    return pl.pallas_call(
        _peer_act_kernel,
        grid=(rows // tb,),
        in_specs=[
            pl.BlockSpec((tb, N_ASSIGN), lambda i: (i, 0), memory_space=pltpu.SMEM),
            pl.BlockSpec((tb * D_CHUNKS, LANES), lambda i: (i, 0)),
            pl.BlockSpec((tb, N_ASSIGN), lambda i: (i, 0)),
            pl.BlockSpec(group.shape, lambda i: (0, 0)),
            pl.BlockSpec(tbl.shape, lambda i: (0, 0), pipeline_mode=pl.Buffered(1)),
        ],
        out_specs=pl.BlockSpec((tb, N_ASSIGN), lambda i: (i, 0)),
        out_shape=jax.ShapeDtypeStruct((rows, N_ASSIGN), jnp.float32),
        scratch_shapes=[pltpu.VMEM((tb, N_ASSIGN * SUBLANES), jnp.float32)] + [
            pltpu.VMEM((N_ASSIGN * HALF_CHUNKS, LANES), jnp.uint32) for _ in range(2 * ACT_GROUP)],
        compiler_params=pltpu.CompilerParams(
            dimension_semantics=("arbitrary",), vmem_limit_bytes=_vmem_limit(tbl.size * 4)),
        name="peer_act",
    )(eid, x3, gate, group, tbl)


def peer_out(eid, w, h, tbl):
    rows = eid.shape[0]
    tb = PEER_ROWS
    assert rows % tb == 0
    return pl.pallas_call(
        _peer_out_kernel,
        grid=(rows // tb,),
        in_specs=[
            pl.BlockSpec((tb, N_ASSIGN), lambda i: (i, 0), memory_space=pltpu.SMEM),
            pl.BlockSpec((tb, N_ASSIGN), lambda i: (i, 0)),
            pl.BlockSpec((tb, D_MODEL), lambda i: (i, 0)),
            pl.BlockSpec(tbl.shape, lambda i: (0, 0), pipeline_mode=pl.Buffered(1)),
        ],
        out_specs=pl.BlockSpec((tb, D_MODEL), lambda i: (i, 0)),
        out_shape=jax.ShapeDtypeStruct((rows, D_MODEL), jnp.float32),
        scratch_shapes=[
            pltpu.VMEM((N_ASSIGN, LANES), jnp.float32),
            pltpu.VMEM((tb * D_CHUNKS, LANES), jnp.float32),
        ],
        compiler_params=pltpu.CompilerParams(
            dimension_semantics=("arbitrary",), vmem_limit_bytes=_vmem_limit(tbl.size * 4)),
        name="peer_out",
    )(eid, w, h, tbl)


def peer_layer(h, g, wqt, keys, u_tbl, v_tbl):
    x3, eid, gate = peer_topk(h, g, wqt, keys)
    w = peer_act(eid, x3, gate, u_tbl)
    return peer_out(eid, w, h, v_tbl)


D_INNER = 2048
SSD_HEAD_DIM = 64
SSD_HEADS = D_INNER // SSD_HEAD_DIM
SSD_GROUPS = 4
HEADS_PER_GROUP = SSD_HEADS // SSD_GROUPS
SSD_STATE = 128
CONV_W = 5
CONV_HALF = (CONV_W - 1) // 2
CONV_DIM = D_INNER + 2 * SSD_GROUPS * SSD_STATE
CHUNK = 128
HALO_ROWS = 16
DENSE_ROWS = 256


def _ssd_in_kernel(h_ref, g_ref, wz_ref, wx_ref, wdt_ref, wdtt_ref, z_ref, xbc_ref, dt_ref, dtt_ref):
    xb = _rms(h_ref[...], g_ref[...]).astype(jnp.bfloat16)
    z_ref[...] = jnp.dot(xb, wz_ref[...], preferred_element_type=jnp.float32).astype(z_ref.dtype)
    xbc_ref[...] = jnp.dot(xb, wx_ref[...], preferred_element_type=jnp.float32).astype(xbc_ref.dtype)
    dt_ref[...] = jnp.dot(xb, wdt_ref[...], preferred_element_type=jnp.float32)
    dtt_ref[...] = lax.dot_general(wdtt_ref[...], xb, (((1,), (1,)), ((), ())),
                                   preferred_element_type=jnp.float32)


def ssd_in(h, g, w_in):
    rows = h.shape[0]
    tm = DENSE_ROWS
    assert rows % tm == 0
    wb = w_in.astype(jnp.bfloat16)
    wz, wx, wdt = wb[:, :D_INNER], wb[:, D_INNER:D_INNER + CONV_DIM], wb[:, D_INNER + CONV_DIM:]
    n_dt = wdt.shape[1]
    once = lambda shape: pl.BlockSpec(shape, lambda i: (0,) * len(shape), pipeline_mode=pl.Buffered(1))
    return pl.pallas_call(
        _ssd_in_kernel,
        grid=(rows // tm,),
        in_specs=[
            pl.BlockSpec((tm, D_MODEL), lambda i: (i, 0)),
            once((1, D_MODEL)), once(wz.shape), once(wx.shape), once(wdt.shape), once((n_dt, D_MODEL)),
        ],
        out_specs=[
            pl.BlockSpec((tm, D_INNER), lambda i: (i, 0)),
            pl.BlockSpec((tm, CONV_DIM), lambda i: (i, 0)),
            pl.BlockSpec((tm, n_dt), lambda i: (i, 0)),
            pl.BlockSpec((n_dt, tm), lambda i: (0, i)),
        ],
        out_shape=[
            jax.ShapeDtypeStruct((rows, D_INNER), jnp.bfloat16),
            jax.ShapeDtypeStruct((rows, CONV_DIM), jnp.bfloat16),
            jax.ShapeDtypeStruct((rows, n_dt), jnp.float32),
            jax.ShapeDtypeStruct((n_dt, rows), jnp.float32),
        ],
        compiler_params=pltpu.CompilerParams(
            dimension_semantics=("arbitrary",), vmem_limit_bytes=_vmem_limit(wb.size * 2)),
        name="ssd_in",
    )(h, g, wz, wx, wdt, wdt.T)


def _dot2(x, m):
    hi = x.astype(jnp.bfloat16)
    lo = (x - hi.astype(jnp.float32)).astype(jnp.bfloat16)
    return (jnp.dot(hi, m, preferred_element_type=jnp.float32)
            + jnp.dot(lo, m, preferred_element_type=jnp.float32))


def _softplus(x):
    return jnp.maximum(x, 0.0) + jnp.log(1.0 + jnp.exp(-jnp.abs(x)))


def _silu(x):
    return x / (1.0 + jnp.exp(-x))


def _ssd_scan_kernel(first_ref, last_ref,
                     prev_ref, cur_ref, next_ref, dt_ref, dtt_ref,
                     convw_ref, convb_ref, bias_ref, biast_ref, alog_ref, alogt_ref, *rest,
                     reverse):
    if reverse:
        (yf_ref, z_ref, h_ref, dskip_ref, gn_ref, wout_ref, o_ref,
         ext_ref, xs_ref, bt_ref, c_ref, y_ref, state_ref) = rest
    else:
        o_ref, ext_ref, xs_ref, bt_ref, c_ref, y_ref, state_ref = rest
    ci = pl.program_id(0)
    n_chunks = pl.num_programs(0)
    chunk_id = (n_chunks - 1 - ci) if reverse else ci
    seq_first = first_ref[chunk_id] == 1
    seq_last = last_ref[chunk_id] == 1
    starts = seq_last if reverse else seq_first

    @pl.when(starts)
    def _():
        state_ref[...] = jnp.zeros_like(state_ref)

    keep_prev = jnp.where(seq_first, 0.0, 1.0)
    keep_next = jnp.where(seq_last, 0.0, 1.0)
    ext_ref[0:SUBLANES, :] = prev_ref[HALO_ROWS - SUBLANES:, :].astype(jnp.float32) * keep_prev
    ext_ref[SUBLANES:SUBLANES + CHUNK, :] = cur_ref[...].astype(jnp.float32)
    ext_ref[SUBLANES + CHUNK:, :] = next_ref[0:SUBLANES, :].astype(jnp.float32) * keep_next
    strip = 4 * LANES
    for s0 in range(0, CONV_DIM, strip):
        acc = jnp.zeros((CHUNK, strip), jnp.float32) + convb_ref[:, s0:s0 + strip]
        for k in range(CONV_W):
            r0 = SUBLANES - CONV_HALF + k
            acc = acc + ext_ref[r0:r0 + CHUNK, s0:s0 + strip] * convw_ref[k:k + 1, s0:s0 + strip]
        act = _silu(acc)
        if s0 < D_INNER:
            xs_ref[:, s0:s0 + strip] = act
        elif s0 < D_INNER + SSD_GROUPS * SSD_STATE:
            for g in range(strip // SSD_STATE):
                gi = (s0 - D_INNER) // SSD_STATE + g
                bt_ref[gi] = act[:, g * SSD_STATE:(g + 1) * SSD_STATE].T.astype(jnp.bfloat16)
        else:
            b0 = s0 - D_INNER - SSD_GROUPS * SSD_STATE
            c_ref[:, b0:b0 + strip] = act.astype(jnp.bfloat16)

    d0 = SSD_HEADS if reverse else 0
    row = lax.broadcasted_iota(jnp.int32, (CHUNK, SSD_HEADS), 0)
    col = lax.broadcasted_iota(jnp.int32, (SSD_HEADS, CHUNK), 1)
    valid = jnp.logical_or(jnp.logical_not(seq_first), row >= META_PAD)
    valid_t = jnp.logical_or(jnp.logical_not(seq_first), col >= META_PAD)
    dt = jnp.where(valid, _softplus(dt_ref[:, d0:d0 + SSD_HEADS] + bias_ref[...]), 0.0)
    dt_t = jnp.where(valid_t, _softplus(dtt_ref[...] + biast_ref[...]), 0.0)
    adt = dt * -jnp.exp(alog_ref[...])
    adt_t = dt_t * -jnp.exp(alogt_ref[...])
    li = lax.broadcasted_iota(jnp.int32, (CHUNK, CHUNK), 0)
    si = lax.broadcasted_iota(jnp.int32, (CHUNK, CHUNK), 1)
    causal = (li <= si) if reverse else (li >= si)
    tri = jnp.where(causal, 1.0, 0.0).astype(jnp.bfloat16)
    tri_t = jnp.where((li >= si) if reverse else (li <= si), 1.0, 0.0).astype(jnp.bfloat16)
    cs = sum(jnp.dot(tri, p, preferred_element_type=jnp.float32) for p in _split3(adt))
    cs_t = sum(jnp.dot(p, tri_t, preferred_element_type=jnp.float32) for p in _split3(adt_t))
    edge = 0 if reverse else CHUNK - 1
    total = cs[edge:edge + 1, :]
    total_t = cs_t[:, edge:edge + 1]
    hp = lax.broadcasted_iota(jnp.int32, (SSD_HEADS, D_INNER), 0)
    hc = lax.broadcasted_iota(jnp.int32, (SSD_HEADS, D_INNER), 1) // SSD_HEAD_DIM
    expand = jnp.where(hp == hc, 1.0, 0.0).astype(jnp.bfloat16)
    xdt = xs_ref[...] * _dot2(dt, expand)
    xdte = (xdt * _dot2(jnp.exp(total - cs), expand)).astype(jnp.bfloat16)
    xdt = xdt.astype(jnp.bfloat16)
    ecs = _dot2(jnp.exp(cs), expand)
    chunk_decay = jnp.exp(total_t)

    for g in range(SSD_GROUPS):
        b_t = bt_ref[g]
        c_g = c_ref[:, g * SSD_STATE:(g + 1) * SSD_STATE]
        cb = jnp.dot(c_g, b_t, preferred_element_type=jnp.float32)
        for hl in range(HEADS_PER_GROUP):
            hh = g * HEADS_PER_GROUP + hl
            ps = slice(hh * SSD_HEAD_DIM, (hh + 1) * SSD_HEAD_DIM)
            seg = cs[:, hh:hh + 1] - cs_t[hh:hh + 1, :]
            w = jnp.where(causal, cb * jnp.exp(jnp.where(causal, seg, 0.0)), 0.0).astype(jnp.bfloat16)
            state = state_ref[hh]
            y = jnp.dot(w, xdt[:, ps], preferred_element_type=jnp.float32)
            y = y + jnp.dot(c_g, state.astype(jnp.bfloat16), preferred_element_type=jnp.float32) * ecs[:, ps]
            y_ref[:, ps] = y
            state_ref[hh] = state * chunk_decay[hh:hh + 1, :] + jnp.dot(
                b_t, xdte[:, ps], preferred_element_type=jnp.float32)

    if not reverse:
        o_ref[...] = y_ref[...].astype(o_ref.dtype)
        return
    y = y_ref[...] + yf_ref[...].astype(jnp.float32) + dskip_ref[...] * xs_ref[...]
    gated = y * _silu(z_ref[...].astype(jnp.float32))
    normed = _rms(gated, gn_ref[...]).astype(jnp.bfloat16)
    out = jnp.dot(normed, wout_ref[...], preferred_element_type=jnp.float32)
    rowd = lax.broadcasted_iota(jnp.int32, out.shape, 0)
    keep = jnp.logical_or(jnp.logical_not(seq_first), rowd >= META_PAD)
    o_ref[...] = jnp.where(keep, h_ref[...] + out, 0.0)


def _chunk_flags(seq_rows):
    first, last = [], []
    for n in seq_rows:
        assert n % CHUNK == 0
        c = n // CHUNK
        first += [1] + [0] * (c - 1)
        last += [0] * (c - 1) + [1]
    return jnp.asarray(first, jnp.int32), jnp.asarray(last, jnp.int32)


def ssd_scan(seq_rows, xbc, dt, dtt, conv_w, conv_b, dt_bias, a_log, reverse, final=None):
    rows = xbc.shape[0]
    n_chunks = rows // CHUNK
    first, last = _chunk_flags(seq_rows)
    d = 1 if reverse else 0
    halo_per_chunk = CHUNK // HALO_ROWS
    n_halo = rows // HALO_ROWS
    cid = (lambda i, *_: n_chunks - 1 - i) if reverse else (lambda i, *_: i)
    rowblk = lambda i, *_: (cid(i), 0)
    const2 = lambda i, *_: (0, 0)
    once = lambda shape: pl.BlockSpec(shape, const2, pipeline_mode=pl.Buffered(1))
    in_specs = [
        pl.BlockSpec((HALO_ROWS, CONV_DIM), lambda i, *_: (jnp.maximum(cid(i) * halo_per_chunk - 1, 0), 0)),
        pl.BlockSpec((CHUNK, CONV_DIM), rowblk),
        pl.BlockSpec((HALO_ROWS, CONV_DIM),
                     lambda i, *_: (jnp.minimum((cid(i) + 1) * halo_per_chunk, n_halo - 1), 0)),
        pl.BlockSpec((CHUNK, 2 * SSD_HEADS), rowblk),
        pl.BlockSpec((SSD_HEADS, CHUNK), lambda i, *_: (d, cid(i))),
        once((CONV_W, CONV_DIM)), once((1, CONV_DIM)),
        once((1, SSD_HEADS)), once((SSD_HEADS, 1)), once((1, SSD_HEADS)), once((SSD_HEADS, 1)),
    ]
    args = [xbc, xbc, xbc, dt, dtt, conv_w, conv_b[None],
            dt_bias[d][None], dt_bias[d][:, None], a_log[d][None], a_log[d][:, None]]
    resident = 0
    if reverse:
        yf, z, h, d_skip, gate_g, w_out = final
        wob = w_out.astype(jnp.bfloat16)
        in_specs += [
            pl.BlockSpec((CHUNK, D_INNER), rowblk), pl.BlockSpec((CHUNK, D_INNER), rowblk),
            pl.BlockSpec((CHUNK, D_MODEL), rowblk),
            once((1, D_INNER)), once((1, D_INNER)), once(wob.shape),
        ]
        args += [yf, z, h, jnp.repeat(d_skip, SSD_HEAD_DIM)[None], gate_g[None], wob]
        out_spec = pl.BlockSpec((CHUNK, D_MODEL), rowblk)
        out_shape = jax.ShapeDtypeStruct((rows, D_MODEL), jnp.float32)
        resident = wob.size * 2
    else:
        out_spec = pl.BlockSpec((CHUNK, D_INNER), rowblk)
        out_shape = jax.ShapeDtypeStruct((rows, D_INNER), jnp.float32)
    return pl.pallas_call(
        functools.partial(_ssd_scan_kernel, reverse=reverse),
        grid_spec=pltpu.PrefetchScalarGridSpec(
            num_scalar_prefetch=2,
            grid=(n_chunks,),
            in_specs=in_specs,
            out_specs=out_spec,
            scratch_shapes=[
                pltpu.VMEM((CHUNK + 2 * SUBLANES, CONV_DIM), jnp.float32),
                pltpu.VMEM((CHUNK, D_INNER), jnp.float32),
                pltpu.VMEM((SSD_GROUPS, SSD_STATE, CHUNK), jnp.bfloat16),
                pltpu.VMEM((CHUNK, SSD_GROUPS * SSD_STATE), jnp.bfloat16),
                pltpu.VMEM((CHUNK, D_INNER), jnp.float32),
                pltpu.VMEM((SSD_HEADS, SSD_STATE, SSD_HEAD_DIM), jnp.float32),
            ],
        ),
        out_shape=out_shape,
        compiler_params=pltpu.CompilerParams(
            dimension_semantics=("arbitrary",), vmem_limit_bytes=_vmem_limit(resident)),
        name="ssd_scan_bwd" if reverse else "ssd_scan_fwd",
    )(first, last, *args)


def ssd_layer(seq_rows, h, g, w_in, conv_w, conv_b, dt_bias, a_log, d_skip, gate_g, w_out):
    z, xbc, dt, dtt = ssd_in(h, g, w_in)
    yf = ssd_scan(seq_rows, xbc, dt, dtt, conv_w, conv_b, dt_bias, a_log, reverse=False)
    return ssd_scan(seq_rows, xbc, dt, dtt, conv_w, conv_b, dt_bias, a_log, reverse=True,
                    final=(yf, z, h, d_skip, gate_g, w_out))


ATTN_HEAD_DIM = 128
N_Q_HEADS = 8
N_KV_HEADS = 2
KV_REP = N_Q_HEADS // N_KV_HEADS
Q_WIDTH = N_Q_HEADS * ATTN_HEAD_DIM
KV_WIDTH = N_KV_HEADS * ATTN_HEAD_DIM
ATTN_SCALE = ATTN_HEAD_DIM ** -0.5
Q_SCALE_LOG2 = float(ATTN_SCALE * np.log2(np.e))
GRID_W = 64
ROPE_THETA = 10000.0
ROPE_FREQS = ATTN_HEAD_DIM // 4
KEY_BLOCK = 128
Q_ROWS = 128


def rope_tables(seq_rows_max):
    r = np.arange(seq_rows_max)
    tok = r - FRONT
    meta = r - META_PAD
    grow = np.where(tok >= 0, tok // GRID_W, -1).astype(np.float32)
    gcol = np.where(tok >= 0, tok % GRID_W, np.maximum(meta, 0)).astype(np.float32)
    inv_freq = jnp.asarray(ROPE_THETA, jnp.float32) ** (-jnp.arange(ROPE_FREQS, dtype=jnp.float32) / ROPE_FREQS)
    ang = jnp.concatenate([jnp.asarray(grow)[:, None] * inv_freq, jnp.asarray(gcol)[:, None] * inv_freq], axis=-1)
    ang = jnp.concatenate([ang, ang], axis=-1)
    sign = np.where(np.arange(ATTN_HEAD_DIM) < ATTN_HEAD_DIM // 2, -1.0, 1.0).astype(np.float32)
    return jnp.cos(ang), jnp.sin(ang) * sign


def _attn_in_kernel(h_ref, g_ref, wq_ref, wkt_ref, wv_ref, qg_ref, kg_ref, cos_ref, sin_ref, cost_ref, sint_ref,
                    q_ref, kt_ref, v_ref):
    rows = h_ref.shape[0]
    xb = _rms(h_ref[...], g_ref[...]).astype(jnp.bfloat16)
    q = jnp.dot(xb, wq_ref[...], preferred_element_type=jnp.float32)
    v_ref[...] = jnp.dot(xb, wv_ref[...], preferred_element_type=jnp.float32).astype(v_ref.dtype)
    kt = lax.dot_general(wkt_ref[...], xb, (((1,), (1,)), ((), ())), preferred_element_type=jnp.float32)
    cos, sin = cos_ref[...], sin_ref[...]
    half = ATTN_HEAD_DIM // 2
    for hd in range(N_Q_HEADS):
        sl = slice(hd * ATTN_HEAD_DIM, (hd + 1) * ATTN_HEAD_DIM)
        x = _rms(q[:, sl], qg_ref[...])
        rot = jnp.concatenate([x[:, half:], x[:, :half]], axis=-1)
        q_ref[:, sl] = ((x * cos + rot * sin) * Q_SCALE_LOG2).astype(q_ref.dtype)
    cos_t, sin_t = cost_ref[...], sint_ref[...]
    for hd in range(N_KV_HEADS):
        x = kt[hd * ATTN_HEAD_DIM:(hd + 1) * ATTN_HEAD_DIM, :]
        x = x * lax.rsqrt(jnp.mean(x * x, axis=0, keepdims=True) + RMS_EPS) * kg_ref[...]
        rot = jnp.concatenate([x[half:, :], x[:half, :]], axis=0)
        x = (x * cos_t + rot * sin_t).astype(kt_ref.dtype)
        for b in range(rows // KEY_BLOCK):
            kt_ref[b, hd * ATTN_HEAD_DIM:(hd + 1) * ATTN_HEAD_DIM, :] = x[:, b * KEY_BLOCK:(b + 1) * KEY_BLOCK]


def attn_in(h, g, w_qkv, q_g, k_g, cos, sin):
    rows = h.shape[0]
    tm = DENSE_ROWS
    assert rows % tm == 0
    wb = w_qkv.astype(jnp.bfloat16)
    wq, wk, wv = wb[:, :Q_WIDTH], wb[:, Q_WIDTH:Q_WIDTH + KV_WIDTH], wb[:, Q_WIDTH + KV_WIDTH:]
    once = lambda shape: pl.BlockSpec(shape, lambda i: (0,) * len(shape), pipeline_mode=pl.Buffered(1))
    return pl.pallas_call(
        _attn_in_kernel,
        grid=(rows // tm,),
        in_specs=[
            pl.BlockSpec((tm, D_MODEL), lambda i: (i, 0)),
            once((1, D_MODEL)), once(wq.shape), once((KV_WIDTH, D_MODEL)), once(wv.shape),
            once((1, ATTN_HEAD_DIM)), once((ATTN_HEAD_DIM, 1)),
            pl.BlockSpec((tm, ATTN_HEAD_DIM), lambda i: (i, 0)),
            pl.BlockSpec((tm, ATTN_HEAD_DIM), lambda i: (i, 0)),
            pl.BlockSpec((ATTN_HEAD_DIM, tm), lambda i: (0, i)),
            pl.BlockSpec((ATTN_HEAD_DIM, tm), lambda i: (0, i)),
        ],
        out_specs=[
            pl.BlockSpec((tm, Q_WIDTH), lambda i: (i, 0)),
            pl.BlockSpec((tm // KEY_BLOCK, KV_WIDTH, KEY_BLOCK), lambda i: (i, 0, 0)),
            pl.BlockSpec((tm, KV_WIDTH), lambda i: (i, 0)),
        ],
        out_shape=[
            jax.ShapeDtypeStruct((rows, Q_WIDTH), jnp.bfloat16),
            jax.ShapeDtypeStruct((rows // KEY_BLOCK, KV_WIDTH, KEY_BLOCK), jnp.bfloat16),
            jax.ShapeDtypeStruct((rows, KV_WIDTH), jnp.bfloat16),
        ],
        compiler_params=pltpu.CompilerParams(
            dimension_semantics=("arbitrary",), vmem_limit_bytes=_vmem_limit(wb.size * 2)),
        name="attn_in",
    )(h, g, wq, wk.T, wv, q_g[None], k_g[:, None], cos, sin, cos.T, sin.T)


def _attn_kernel(q_ref, kt_ref, v_ref, h_ref, wo_ref, o_ref, o_acc, *, key_blocks):
    n_blocks = kt_ref.shape[0]
    n_steps = (n_blocks - 1) // key_blocks
    stacked = KV_REP * Q_ROWS
    qs = [jnp.concatenate([q_ref[:, (g * KV_REP + r) * ATTN_HEAD_DIM:(g * KV_REP + r + 1) * ATTN_HEAD_DIM]
                           for r in range(KV_REP)], axis=0) for g in range(N_KV_HEADS)]

    def step(block0, blocks, carry, mask_front):
        width = blocks * KEY_BLOCK
        ones_col = jnp.where(lax.broadcasted_iota(jnp.int32, (width, LANES), 1) == 0, 1.0, 0.0
                             ).astype(jnp.bfloat16)
        new = []
        for g in range(N_KV_HEADS):
            m, l, acc = carry[g]
            ks = slice(g * ATTN_HEAD_DIM, (g + 1) * ATTN_HEAD_DIM)
            kt = jnp.concatenate([kt_ref[block0 + b, ks, :] for b in range(blocks)], axis=1)
            s = jnp.dot(qs[g], kt, preferred_element_type=jnp.float32)
            if mask_front:
                s = jnp.where(lax.broadcasted_iota(jnp.int32, (1, width), 1) >= META_PAD, s, NEG_INF)
            m_new = jnp.maximum(m, jnp.max(s, axis=-1, keepdims=True))
            alpha = jnp.exp2(m - m_new)
            p = jnp.exp2(s - m_new).astype(jnp.bfloat16)
            v = v_ref[pl.ds(pl.multiple_of(block0 * KEY_BLOCK, KEY_BLOCK), width), ks]
            r = jnp.dot(p, jnp.concatenate([v, ones_col], axis=1), preferred_element_type=jnp.float32)
            new.append((m_new, alpha * l + r[:, ATTN_HEAD_DIM:ATTN_HEAD_DIM + 1],
                        alpha * acc + r[:, :ATTN_HEAD_DIM]))
        return tuple(new)

    init = tuple((jnp.full((stacked, 1), NEG_INF, jnp.float32), jnp.zeros((stacked, 1), jnp.float32),
                  jnp.zeros((stacked, ATTN_HEAD_DIM), jnp.float32)) for _ in range(N_KV_HEADS))
    carry = step(0, 1, init, True)
    carry = lax.fori_loop(0, n_steps, lambda st, c: step(1 + st * key_blocks, key_blocks, c, False), carry)
    for g in range(N_KV_HEADS):
        _, l, acc = carry[g]
        out = (acc / l).astype(jnp.bfloat16)
        for r in range(KV_REP):
            hd = g * KV_REP + r
            o_acc[:, hd * ATTN_HEAD_DIM:(hd + 1) * ATTN_HEAD_DIM] = out[r * Q_ROWS:(r + 1) * Q_ROWS]
    o_ref[...] = h_ref[...] + jnp.dot(o_acc[...], wo_ref[...], preferred_element_type=jnp.float32)


MAX_KEY_BLOCKS = 8


def _key_blocks_per_step(n_blocks):
    return max(d for d in range(1, MAX_KEY_BLOCKS + 1) if (n_blocks - 1) % d == 0)


def attention(n_seq, q, kt, v, h, w_o):
    rows = q.shape[0]
    seq = rows // n_seq
    n_blocks = seq // KEY_BLOCK
    qb = seq // Q_ROWS
    wob = w_o.astype(jnp.bfloat16)
    kernel_fn = functools.partial(_attn_kernel, key_blocks=_key_blocks_per_step(n_blocks))
    return pl.pallas_call(
        kernel_fn,
        grid=(n_seq, qb),
        in_specs=[
            pl.BlockSpec((Q_ROWS, Q_WIDTH), lambda b, i: (b * qb + i, 0)),
            pl.BlockSpec((n_blocks, KV_WIDTH, KEY_BLOCK), lambda b, i: (b, 0, 0), pipeline_mode=pl.Buffered(1)),
            pl.BlockSpec((seq, KV_WIDTH), lambda b, i: (b, 0), pipeline_mode=pl.Buffered(1)),
            pl.BlockSpec((Q_ROWS, D_MODEL), lambda b, i: (b * qb + i, 0)),
            pl.BlockSpec(wob.shape, lambda b, i: (0, 0), pipeline_mode=pl.Buffered(1)),
        ],
        out_specs=pl.BlockSpec((Q_ROWS, D_MODEL), lambda b, i: (b * qb + i, 0)),
        out_shape=jax.ShapeDtypeStruct((rows, D_MODEL), jnp.float32),
        scratch_shapes=[pltpu.VMEM((Q_ROWS, Q_WIDTH), jnp.bfloat16)],
        compiler_params=pltpu.CompilerParams(
            dimension_semantics=("arbitrary", "arbitrary"),
            vmem_limit_bytes=_vmem_limit(2 * seq * KV_WIDTH * 2 + wob.size * 2)),
        name="attention",
    )(q, kt, v, h, wob)


def _to_rows(x, meta):
    b, s, d = x.shape
    front = jnp.concatenate([jnp.zeros((META_PAD, d), x.dtype), meta.astype(x.dtype)], axis=0)
    rows = jnp.concatenate([jnp.broadcast_to(front[None], (b, FRONT, d)), x], axis=1)
    return rows.reshape(b * (FRONT + s), d)


def kernel(x_prompt, x_sample, meta_tokens, norm_mix_g, norm_ffn_g, ssd_w_in, ssd_conv_w, ssd_conv_b, ssd_dt_bias,
           ssd_a_log, ssd_d_skip, ssd_gate_norm_g, ssd_w_out, attn_w_qkv, attn_q_norm_g, attn_k_norm_g, attn_w_o,
           peer_w_q, peer_sub_keys, peer_u, peer_v):
    trunks = (x_prompt, x_sample)
    seq_rows = [FRONT + x.shape[1] for x in trunks for _ in range(x.shape[0])]
    trunk_rows = [x.shape[0] * (FRONT + x.shape[1]) for x in trunks]
    h = jnp.concatenate([_to_rows(x, meta_tokens) for x in trunks], axis=0)

    cos, sin = rope_tables(max(seq_rows))
    cos_all = jnp.concatenate([cos[:n] for n in seq_rows], axis=0)
    sin_all = jnp.concatenate([sin[:n] for n in seq_rows], axis=0)

    depth = norm_mix_g.shape[0]
    for i in range(depth):
        j = i // 2
        g_mix = norm_mix_g[i][None]
        if i % 2 == 0:
            h = ssd_layer(seq_rows, h, g_mix, ssd_w_in[j], ssd_conv_w[j], ssd_conv_b[j], ssd_dt_bias[j],
                          ssd_a_log[j], ssd_d_skip[j], ssd_gate_norm_g[j], ssd_w_out[j])
        else:
            q, kt, v = attn_in(h, g_mix, attn_w_qkv[j], attn_q_norm_g[j], attn_k_norm_g[j], cos_all, sin_all)
            parts, r0 = [], 0
            for x, n in zip(trunks, trunk_rows):
                r1 = r0 + n
                parts.append(attention(x.shape[0], q[r0:r1], kt[r0 // KEY_BLOCK:r1 // KEY_BLOCK], v[r0:r1],
                                       h[r0:r1], attn_w_o[j]))
                r0 = r1
            h = jnp.concatenate(parts, axis=0)
        wqt = peer_w_q[i].T.astype(jnp.bfloat16)
        keys = peer_sub_keys[i].reshape(2 * PEER_HEADS, N_KEYS, PEER_HALF).astype(jnp.bfloat16)
        h = peer_layer(h, norm_ffn_g[i][None], wqt, keys,
                       pack_expert_table(peer_u[i]), pack_expert_table(peer_v[i]))

    outs, r0 = [], 0
    for x, n in zip(trunks, trunk_rows):
        b, s, d = x.shape
        outs.append(h[r0:r0 + n].reshape(b, FRONT + s, d)[:, FRONT:])
        r0 += n
    return tuple(outs)
```

```python
import functools

import jax
import jax.numpy as jnp
import numpy as np
from jax import lax
from jax.experimental import pallas as pl
from jax.experimental.pallas import tpu as pltpu

LANES = 128
SUBLANES = 8
VREG_ELEMS = LANES * SUBLANES
V7X_VMEM_BYTES = 64 * 1024 * 1024

D_MODEL = 1024
N_META = 16
FRONT = 128
META_PAD = FRONT - N_META
RMS_EPS = 1e-6

PEER_HEADS = 8
N_KEYS = 128
PEER_HALF = 128
PEER_TOPK = 16
N_ASSIGN = PEER_HEADS * PEER_TOPK
N_EXPERTS = N_KEYS * N_KEYS
D_CHUNKS = D_MODEL // LANES
HALF_CHUNKS = D_CHUNKS // 2

NEG_INF = float("-inf")


def _vmem_limit(resident_bytes):
    return int(min(V7X_VMEM_BYTES - 8 * 1024 * 1024, resident_bytes + 20 * 1024 * 1024))


def _rms(x, g):
    return x * lax.rsqrt(jnp.mean(x * x, axis=-1, keepdims=True) + RMS_EPS) * g


PEER_Q_ROWS = 256
TOPK_LOCKSTEP = 4
PAIRS = tuple((i, j) for i in range(PEER_TOPK) for j in range(PEER_TOPK) if (i + 1) * (j + 1) <= PEER_TOPK)
N_PAIRS = len(PAIRS)
PAIR_ROWS = -(-N_PAIRS // SUBLANES) * SUBLANES


def _top16_rows(score_list, n_rows):
    rid = lax.broadcasted_iota(jnp.int32, score_list[0].shape, 0).astype(jnp.float32)
    score_list = list(score_list)
    vals = [[] for _ in score_list]
    ids = [[] for _ in score_list]
    for _ in range(PEER_TOPK):
        for k, s in enumerate(score_list):
            m = jnp.max(s, axis=0, keepdims=True)
            i = jnp.min(jnp.where(s == m, rid, float(n_rows)), axis=0, keepdims=True)
            vals[k].append(m)
            ids[k].append(i)
            score_list[k] = jnp.where(rid == i, NEG_INF, s)
    return [(jnp.concatenate(v, axis=0), jnp.concatenate(i, axis=0)) for v, i in zip(vals, ids)]


def _pair_selectors():
    sel0 = np.zeros((PAIR_ROWS, PEER_TOPK), np.float32)
    sel1 = np.zeros((PAIR_ROWS, PEER_TOPK), np.float32)
    for r, (i, j) in enumerate(PAIRS):
        sel0[r, i] = 1.0
        sel1[r, j] = 1.0
    return jnp.asarray(sel0, jnp.bfloat16), jnp.asarray(sel1, jnp.bfloat16)


def _peer_topk_kernel(h_ref, g_ref, wqt_ref, keys_ref, sel0_ref, sel1_ref, x3_ref, eid_ref, gate_ref):
    rows = h_ref.shape[0]
    x = _rms(h_ref[...], g_ref[...])
    for c in range(D_CHUNKS):
        x3_ref[pl.ds(c, rows, stride=D_CHUNKS), :] = x[:, c * LANES:(c + 1) * LANES]
    xb = x.astype(jnp.bfloat16)
    eids, gates = [], []
    for hd0 in range(0, PEER_HEADS, TOPK_LOCKSTEP):
        heads = range(hd0, hd0 + TOPK_LOCKSTEP)
        scores = []
        for j in range(2 * hd0, 2 * (hd0 + TOPK_LOCKSTEP)):
            qt = lax.dot_general(wqt_ref[j * PEER_HALF:(j + 1) * PEER_HALF, :], xb,
                                 (((1,), (1,)), ((), ())), preferred_element_type=jnp.float32)
            scores.append(jnp.dot(keys_ref[j], qt.astype(jnp.bfloat16), preferred_element_type=jnp.float32))
        lists = _top16_rows(scores, N_KEYS)
        cands, cidxs = [], []
        for k in range(TOPK_LOCKSTEP):
            (sv0, si0), (sv1, si1) = lists[2 * k], lists[2 * k + 1]
            cand = _dot3(sel0_ref[...], sv0, left=True) + _dot3(sel1_ref[...], sv1, left=True)
            pos = lax.broadcasted_iota(jnp.int32, cand.shape, 0).astype(jnp.float32)
            cands.append(jnp.where(pos < N_PAIRS, cand, NEG_INF))
            cidxs.append(
                jnp.dot(sel0_ref[...], si0.astype(jnp.bfloat16), preferred_element_type=jnp.float32) * N_KEYS
                + jnp.dot(sel1_ref[...], si1.astype(jnp.bfloat16), preferred_element_type=jnp.float32))
        tops = [[] for _ in heads]
        sels = [[] for _ in heads]
        for _ in range(PEER_TOPK):
            for k in range(TOPK_LOCKSTEP):
                m = jnp.max(cands[k], axis=0, keepdims=True)
                p = jnp.min(jnp.where(cands[k] == m, pos, float(PAIR_ROWS)), axis=0, keepdims=True)
                hit = pos == p
                tops[k].append(m)
                sels[k].append(jnp.sum(jnp.where(hit, cidxs[k], 0.0), axis=0, keepdims=True))
                cands[k] = jnp.where(hit, NEG_INF, cands[k])
        for k in range(TOPK_LOCKSTEP):
            top = jnp.concatenate(tops[k], axis=0)
            e = jnp.exp(top - top[0:1])
            gates.append(e / jnp.sum(e, axis=0, keepdims=True))
            eids.append(jnp.concatenate(sels[k], axis=0))
    gate_t = jnp.concatenate(gates, axis=0)
    eid_t = jnp.concatenate(eids, axis=0)
    for b in range(rows // LANES):
        sl = slice(b * LANES, (b + 1) * LANES)
        gate_ref[sl, :] = gate_t[:, sl].T
        eid_ref[sl, :] = eid_t[:, sl].T.astype(jnp.int32) * HALF_CHUNKS


def peer_topk(h, g, wqt, keys):
    rows = h.shape[0]
    tb = PEER_Q_ROWS
    assert rows % tb == 0
    const2 = lambda i: (0, 0)
    sel0, sel1 = _pair_selectors()
    return pl.pallas_call(
        _peer_topk_kernel,
        grid=(rows // tb,),
        in_specs=[
            pl.BlockSpec((tb, D_MODEL), lambda i: (i, 0)),
            pl.BlockSpec((1, D_MODEL), const2),
            pl.BlockSpec(wqt.shape, const2, pipeline_mode=pl.Buffered(1)),
            pl.BlockSpec(keys.shape, lambda i: (0, 0, 0), pipeline_mode=pl.Buffered(1)),
            pl.BlockSpec(sel0.shape, const2),
            pl.BlockSpec(sel1.shape, const2),
        ],
        out_specs=[
            pl.BlockSpec((tb * D_CHUNKS, LANES), lambda i: (i, 0)),
            pl.BlockSpec((tb, N_ASSIGN), lambda i: (i, 0)),
            pl.BlockSpec((tb, N_ASSIGN), lambda i: (i, 0)),
        ],
        out_shape=[
            jax.ShapeDtypeStruct((rows * D_CHUNKS, LANES), jnp.float32),
            jax.ShapeDtypeStruct((rows, N_ASSIGN), jnp.int32),
            jax.ShapeDtypeStruct((rows, N_ASSIGN), jnp.float32),
        ],
        compiler_params=pltpu.CompilerParams(
            dimension_semantics=("arbitrary",),
            vmem_limit_bytes=_vmem_limit(wqt.size * 2 + keys.size * 2)),
        name="peer_topk",
    )(h, g, wqt, keys, sel0, sel1)


PEER_ROWS = 128
HIGH_HALF = 0xFFFF0000


def _unpack_pair(words):
    return (lax.bitcast_convert_type(words << 16, jnp.float32),
            lax.bitcast_convert_type(words & jnp.uint32(HIGH_HALF), jnp.float32))


def pack_expert_table(w):
    b = lax.bitcast_convert_type(w.astype(jnp.bfloat16), jnp.uint16).astype(jnp.uint32)
    b = b.reshape(N_EXPERTS, 2, HALF_CHUNKS, LANES)
    return (b[:, 0] | (b[:, 1] << 16)).reshape(N_EXPERTS * HALF_CHUNKS, LANES)


def _split3(x):
    hi = x.astype(jnp.bfloat16)
    r = x - hi.astype(jnp.float32)
    mid = r.astype(jnp.bfloat16)
    lo = (r - mid.astype(jnp.float32)).astype(jnp.bfloat16)
    return hi, mid, lo


def _dot3(x, m, left=False):
    if left:
        return sum(jnp.dot(x, p, preferred_element_type=jnp.float32) for p in _split3(m))
    return sum(jnp.dot(p, m, preferred_element_type=jnp.float32) for p in _split3(x))


ACT_GROUP = 4
PAIR_CHUNK_OF_ROW = tuple((q % 2) * HALF_CHUNKS + q // 2 for q in range(D_CHUNKS))


def _gather_tile(eid_ref, t, tbl_ref, tile_ref):
    for a in range(N_ASSIGN):
        e4 = pl.multiple_of(eid_ref[t, a], HALF_CHUNKS)
        tile_ref[a * HALF_CHUNKS:(a + 1) * HALF_CHUNKS, :] = tbl_ref[pl.ds(e4, HALF_CHUNKS), :]


def _for_each_token(rows, gather, compute, tiles_a, tiles_b):
    group = len(tiles_a)
    n_groups = rows // group

    def gather_group(j, tiles):
        for k, tile in enumerate(tiles):
            gather(j * group + k, tile)

    def compute_group(j, tiles):
        for k, tile in enumerate(tiles):
            compute(j * group + k, tile)

    gather_group(0, tiles_a)

    def pair(i, carry):
        j = 2 * i
        compute_group(j, tiles_a)
        gather_group(j + 1, tiles_b)

        @pl.when(j + 1 < n_groups)
        def _():
            compute_group(j + 1, tiles_b)
            gather_group(jnp.minimum(j + 2, n_groups - 1), tiles_a)

        return carry

    lax.fori_loop(0, n_groups // 2, pair, 0)


def _peer_act_kernel(eid_ref, x3_ref, gate_ref, group_ref, tbl_ref, w_ref, part_ref, *tiles):
    rows = gate_ref.shape[0]
    q = lax.broadcasted_iota(jnp.int32, (SUBLANES, N_ASSIGN * SUBLANES), 0)
    n = lax.broadcasted_iota(jnp.int32, (SUBLANES, N_ASSIGN * SUBLANES), 1)
    mask = (n % SUBLANES) == q
    zeros = jnp.zeros((SUBLANES, LANES), jnp.bfloat16)

    def compute(t, tile_ref):
        xt = x3_ref[pl.ds(pl.multiple_of(t * D_CHUNKS, D_CHUNKS), D_CHUNKS), :]
        xp = jnp.concatenate([xt[c:c + 1] for c in PAIR_CHUNK_OF_ROW], axis=0).astype(jnp.bfloat16)
        lhs = jnp.concatenate([xp, zeros], axis=0)
        u = pltpu.bitcast(tile_ref[...], jnp.bfloat16)
        r = lax.dot_general(lhs, u, (((1,), (1,)), ((), ())), preferred_element_type=jnp.float32)
        part_ref[pl.ds(t, 1), :] = jnp.sum(jnp.where(mask, r[0:SUBLANES], 0.0), axis=0, keepdims=True)

    gather = lambda t, tile_ref: _gather_tile(eid_ref, t, tbl_ref, tile_ref)
    _for_each_token(rows, gather, compute, tiles[:ACT_GROUP], tiles[ACT_GROUP:])
    act = _dot3(part_ref[...], group_ref[...])
    gelu = 0.5 * act * (1.0 + lax.erf(act * np.float32(1.0 / np.sqrt(2.0))))
    w_ref[...] = gate_ref[...] * gelu


def _peer_out_kernel(eid_ref, w_ref, h_ref, tbl_ref, o_ref, wrep_ref, acc_ref):
    rows = w_ref.shape[0]

    def expand(t):
        return jnp.broadcast_to(w_ref[pl.ds(t, 1), :], (LANES, N_ASSIGN)).T

    def token(t, wrep):
        wrep_ref[...] = wrep
        wrep_next = expand(jnp.minimum(t + 1, rows - 1))
        lo = [jnp.zeros((HALF_CHUNKS, LANES), jnp.float32) for _ in range(2)]
        hi = [jnp.zeros((HALF_CHUNKS, LANES), jnp.float32) for _ in range(2)]
        for a in range(N_ASSIGN):
            e4 = pl.multiple_of(eid_ref[t, a], HALF_CHUNKS)
            row_lo, row_hi = _unpack_pair(tbl_ref[pl.ds(e4, HALF_CHUNKS), :])
            wv = wrep_ref[pl.ds(a, 1), :]
            lo[a % 2] = lo[a % 2] + wv * row_lo
            hi[a % 2] = hi[a % 2] + wv * row_hi
        base = pl.multiple_of(t * D_CHUNKS, D_CHUNKS)
        acc_ref[pl.ds(base, HALF_CHUNKS), :] = lo[0] + lo[1]
        acc_ref[pl.ds(base + HALF_CHUNKS, HALF_CHUNKS), :] = hi[0] + hi[1]
        return wrep_next

    lax.fori_loop(0, rows, token, expand(0))
    for c in range(D_CHUNKS):
        sl = slice(c * LANES, (c + 1) * LANES)
        o_ref[:, sl] = h_ref[:, sl] + acc_ref[pl.ds(c, rows, stride=D_CHUNKS), :]


def _group_matrix():
    n = np.arange(N_ASSIGN * SUBLANES)
    return jnp.asarray((n[:, None] // SUBLANES) == np.arange(N_ASSIGN)[None, :], jnp.bfloat16)


def peer_act(eid, x3, gate, tbl):
    rows = eid.shape[0]
    tb = PEER_ROWS
    assert rows % tb == 0
    group = _group_matrix()
    return pl.pallas_call(
        _peer_act_kernel,
        grid=(rows // tb,),
        in_specs=[
            pl.BlockSpec((tb, N_ASSIGN), lambda i: (i, 0), memory_space=pltpu.SMEM),
            pl.BlockSpec((tb * D_CHUNKS, LANES), lambda i: (i, 0)),
            pl.BlockSpec((tb, N_ASSIGN), lambda i: (i, 0)),
            pl.BlockSpec(group.shape, lambda i: (0, 0)),
            pl.BlockSpec(tbl.shape, lambda i: (0, 0), pipeline_mode=pl.Buffered(1)),
        ],
        out_specs=pl.BlockSpec((tb, N_ASSIGN), lambda i: (i, 0)),
        out_shape=jax.ShapeDtypeStruct((rows, N_ASSIGN), jnp.float32),
        scratch_shapes=[pltpu.VMEM((tb, N_ASSIGN * SUBLANES), jnp.float32)] + [
            pltpu.VMEM((N_ASSIGN * HALF_CHUNKS, LANES), jnp.uint32) for _ in range(2 * ACT_GROUP)],
        compiler_params=pltpu.CompilerParams(
            dimension_semantics=("arbitrary",), vmem_limit_bytes=_vmem_limit(tbl.size * 4)),
        name="peer_act",
    )(eid, x3, gate, group, tbl)


def peer_out(eid, w, h, tbl):
    rows = eid.shape[0]
    tb = PEER_ROWS
    assert rows % tb == 0
    return pl.pallas_call(
        _peer_out_kernel,
        grid=(rows // tb,),
        in_specs=[
            pl.BlockSpec((tb, N_ASSIGN), lambda i: (i, 0), memory_space=pltpu.SMEM),
            pl.BlockSpec((tb, N_ASSIGN), lambda i: (i, 0)),
            pl.BlockSpec((tb, D_MODEL), lambda i: (i, 0)),
            pl.BlockSpec(tbl.shape, lambda i: (0, 0), pipeline_mode=pl.Buffered(1)),
        ],
        out_specs=pl.BlockSpec((tb, D_MODEL), lambda i: (i, 0)),
        out_shape=jax.ShapeDtypeStruct((rows, D_MODEL), jnp.float32),
        scratch_shapes=[
            pltpu.VMEM((N_ASSIGN, LANES), jnp.float32),
            pltpu.VMEM((tb * D_CHUNKS, LANES), jnp.float32),
        ],
        compiler_params=pltpu.CompilerParams(
            dimension_semantics=("arbitrary",), vmem_limit_bytes=_vmem_limit(tbl.size * 4)),
        name="peer_out",
    )(eid, w, h, tbl)


def peer_layer(h, g, wqt, keys, u_tbl, v_tbl):
    x3, eid, gate = peer_topk(h, g, wqt, keys)
    w = peer_act(eid, x3, gate, u_tbl)
    return peer_out(eid, w, h, v_tbl)


D_INNER = 2048
SSD_HEAD_DIM = 64
SSD_HEADS = D_INNER // SSD_HEAD_DIM
SSD_GROUPS = 4
HEADS_PER_GROUP = SSD_HEADS // SSD_GROUPS
SSD_STATE = 128
CONV_W = 5
CONV_HALF = (CONV_W - 1) // 2
CONV_DIM = D_INNER + 2 * SSD_GROUPS * SSD_STATE
CHUNK = 128
HALO_ROWS = 16
DENSE_ROWS = 256


def _ssd_in_kernel(h_ref, g_ref, wz_ref, wx_ref, wdt_ref, wdtt_ref, z_ref, xbc_ref, dt_ref, dtt_ref):
    xb = _rms(h_ref[...], g_ref[...]).astype(jnp.bfloat16)
    z_ref[...] = jnp.dot(xb, wz_ref[...], preferred_element_type=jnp.float32).astype(z_ref.dtype)
    xbc_ref[...] = jnp.dot(xb, wx_ref[...], preferred_element_type=jnp.float32).astype(xbc_ref.dtype)
    dt_ref[...] = jnp.dot(xb, wdt_ref[...], preferred_element_type=jnp.float32)
    dtt_ref[...] = lax.dot_general(wdtt_ref[...], xb, (((1,), (1,)), ((), ())),
                                   preferred_element_type=jnp.float32)


def ssd_in(h, g, w_in):
    rows = h.shape[0]
    tm = DENSE_ROWS
    assert rows % tm == 0
    wb = w_in.astype(jnp.bfloat16)
    wz, wx, wdt = wb[:, :D_INNER], wb[:, D_INNER:D_INNER + CONV_DIM], wb[:, D_INNER + CONV_DIM:]
    n_dt = wdt.shape[1]
    once = lambda shape: pl.BlockSpec(shape, lambda i: (0,) * len(shape), pipeline_mode=pl.Buffered(1))
    return pl.pallas_call(
        _ssd_in_kernel,
        grid=(rows // tm,),
        in_specs=[
            pl.BlockSpec((tm, D_MODEL), lambda i: (i, 0)),
            once((1, D_MODEL)), once(wz.shape), once(wx.shape), once(wdt.shape), once((n_dt, D_MODEL)),
        ],
        out_specs=[
            pl.BlockSpec((tm, D_INNER), lambda i: (i, 0)),
            pl.BlockSpec((tm, CONV_DIM), lambda i: (i, 0)),
            pl.BlockSpec((tm, n_dt), lambda i: (i, 0)),
            pl.BlockSpec((n_dt, tm), lambda i: (0, i)),
        ],
        out_shape=[
            jax.ShapeDtypeStruct((rows, D_INNER), jnp.bfloat16),
            jax.ShapeDtypeStruct((rows, CONV_DIM), jnp.bfloat16),
            jax.ShapeDtypeStruct((rows, n_dt), jnp.float32),
            jax.ShapeDtypeStruct((n_dt, rows), jnp.float32),
        ],
        compiler_params=pltpu.CompilerParams(
            dimension_semantics=("arbitrary",), vmem_limit_bytes=_vmem_limit(wb.size * 2)),
        name="ssd_in",
    )(h, g, wz, wx, wdt, wdt.T)


def _dot2(x, m):
    hi = x.astype(jnp.bfloat16)
    lo = (x - hi.astype(jnp.float32)).astype(jnp.bfloat16)
    return (jnp.dot(hi, m, preferred_element_type=jnp.float32)
            + jnp.dot(lo, m, preferred_element_type=jnp.float32))


def _softplus(x):
    return jnp.maximum(x, 0.0) + jnp.log(1.0 + jnp.exp(-jnp.abs(x)))


def _silu(x):
    return x / (1.0 + jnp.exp(-x))


def _ssd_scan_kernel(first_ref, last_ref,
                     prev_ref, cur_ref, next_ref, dt_ref, dtt_ref,
                     convw_ref, convb_ref, bias_ref, biast_ref, alog_ref, alogt_ref, *rest,
                     reverse):
    if reverse:
        (yf_ref, z_ref, h_ref, dskip_ref, gn_ref, wout_ref, o_ref,
         ext_ref, xs_ref, bt_ref, c_ref, y_ref, state_ref) = rest
    else:
        o_ref, ext_ref, xs_ref, bt_ref, c_ref, y_ref, state_ref = rest
    ci = pl.program_id(0)
    n_chunks = pl.num_programs(0)
    chunk_id = (n_chunks - 1 - ci) if reverse else ci
    seq_first = first_ref[chunk_id] == 1
    seq_last = last_ref[chunk_id] == 1
    starts = seq_last if reverse else seq_first

    @pl.when(starts)
    def _():
        state_ref[...] = jnp.zeros_like(state_ref)

    keep_prev = jnp.where(seq_first, 0.0, 1.0)
    keep_next = jnp.where(seq_last, 0.0, 1.0)
    ext_ref[0:SUBLANES, :] = prev_ref[HALO_ROWS - SUBLANES:, :].astype(jnp.float32) * keep_prev
    ext_ref[SUBLANES:SUBLANES + CHUNK, :] = cur_ref[...].astype(jnp.float32)
    ext_ref[SUBLANES + CHUNK:, :] = next_ref[0:SUBLANES, :].astype(jnp.float32) * keep_next
    strip = 4 * LANES
    for s0 in range(0, CONV_DIM, strip):
        acc = jnp.zeros((CHUNK, strip), jnp.float32) + convb_ref[:, s0:s0 + strip]
        for k in range(CONV_W):
            r0 = SUBLANES - CONV_HALF + k
            acc = acc + ext_ref[r0:r0 + CHUNK, s0:s0 + strip] * convw_ref[k:k + 1, s0:s0 + strip]
        act = _silu(acc)
        if s0 < D_INNER:
            xs_ref[:, s0:s0 + strip] = act
        elif s0 < D_INNER + SSD_GROUPS * SSD_STATE:
            for g in range(strip // SSD_STATE):
                gi = (s0 - D_INNER) // SSD_STATE + g
                bt_ref[gi] = act[:, g * SSD_STATE:(g + 1) * SSD_STATE].T.astype(jnp.bfloat16)
        else:
            b0 = s0 - D_INNER - SSD_GROUPS * SSD_STATE
            c_ref[:, b0:b0 + strip] = act.astype(jnp.bfloat16)

    d0 = SSD_HEADS if reverse else 0
    row = lax.broadcasted_iota(jnp.int32, (CHUNK, SSD_HEADS), 0)
    col = lax.broadcasted_iota(jnp.int32, (SSD_HEADS, CHUNK), 1)
    valid = jnp.logical_or(jnp.logical_not(seq_first), row >= META_PAD)
    valid_t = jnp.logical_or(jnp.logical_not(seq_first), col >= META_PAD)
    dt = jnp.where(valid, _softplus(dt_ref[:, d0:d0 + SSD_HEADS] + bias_ref[...]), 0.0)
    dt_t = jnp.where(valid_t, _softplus(dtt_ref[...] + biast_ref[...]), 0.0)
    adt = dt * -jnp.exp(alog_ref[...])
    adt_t = dt_t * -jnp.exp(alogt_ref[...])
    li = lax.broadcasted_iota(jnp.int32, (CHUNK, CHUNK), 0)
    si = lax.broadcasted_iota(jnp.int32, (CHUNK, CHUNK), 1)
    causal = (li <= si) if reverse else (li >= si)
    tri = jnp.where(causal, 1.0, 0.0).astype(jnp.bfloat16)
    tri_t = jnp.where((li >= si) if reverse else (li <= si), 1.0, 0.0).astype(jnp.bfloat16)
    cs = sum(jnp.dot(tri, p, preferred_element_type=jnp.float32) for p in _split3(adt))
    cs_t = sum(jnp.dot(p, tri_t, preferred_element_type=jnp.float32) for p in _split3(adt_t))
    edge = 0 if reverse else CHUNK - 1
    total = cs[edge:edge + 1, :]
    total_t = cs_t[:, edge:edge + 1]
    hp = lax.broadcasted_iota(jnp.int32, (SSD_HEADS, D_INNER), 0)
    hc = lax.broadcasted_iota(jnp.int32, (SSD_HEADS, D_INNER), 1) // SSD_HEAD_DIM
    expand = jnp.where(hp == hc, 1.0, 0.0).astype(jnp.bfloat16)
    xdt = xs_ref[...] * _dot2(dt, expand)
    xdte = (xdt * _dot2(jnp.exp(total - cs), expand)).astype(jnp.bfloat16)
    xdt = xdt.astype(jnp.bfloat16)
    ecs = _dot2(jnp.exp(cs), expand)
    chunk_decay = jnp.exp(total_t)

    for g in range(SSD_GROUPS):
        b_t = bt_ref[g]
        c_g = c_ref[:, g * SSD_STATE:(g + 1) * SSD_STATE]
        cb = jnp.dot(c_g, b_t, preferred_element_type=jnp.float32)
        for hl in range(HEADS_PER_GROUP):
            hh = g * HEADS_PER_GROUP + hl
            ps = slice(hh * SSD_HEAD_DIM, (hh + 1) * SSD_HEAD_DIM)
            seg = cs[:, hh:hh + 1] - cs_t[hh:hh + 1, :]
            w = jnp.where(causal, cb * jnp.exp(jnp.where(causal, seg, 0.0)), 0.0).astype(jnp.bfloat16)
            state = state_ref[hh]
            y = jnp.dot(w, xdt[:, ps], preferred_element_type=jnp.float32)
            y = y + jnp.dot(c_g, state.astype(jnp.bfloat16), preferred_element_type=jnp.float32) * ecs[:, ps]
            y_ref[:, ps] = y
            state_ref[hh] = state * chunk_decay[hh:hh + 1, :] + jnp.dot(
                b_t, xdte[:, ps], preferred_element_type=jnp.float32)

    if not reverse:
        o_ref[...] = y_ref[...].astype(o_ref.dtype)
        return
    y = y_ref[...] + yf_ref[...].astype(jnp.float32) + dskip_ref[...] * xs_ref[...]
    gated = y * _silu(z_ref[...].astype(jnp.float32))
    normed = _rms(gated, gn_ref[...]).astype(jnp.bfloat16)
    out = jnp.dot(normed, wout_ref[...], preferred_element_type=jnp.float32)
    rowd = lax.broadcasted_iota(jnp.int32, out.shape, 0)
    keep = jnp.logical_or(jnp.logical_not(seq_first), rowd >= META_PAD)
    o_ref[...] = jnp.where(keep, h_ref[...] + out, 0.0)


def _chunk_flags(seq_rows):
    first, last = [], []
    for n in seq_rows:
        assert n % CHUNK == 0
        c = n // CHUNK
        first += [1] + [0] * (c - 1)
        last += [0] * (c - 1) + [1]
    return jnp.asarray(first, jnp.int32), jnp.asarray(last, jnp.int32)


def ssd_scan(seq_rows, xbc, dt, dtt, conv_w, conv_b, dt_bias, a_log, reverse, final=None):
    rows = xbc.shape[0]
    n_chunks = rows // CHUNK
    first, last = _chunk_flags(seq_rows)
    d = 1 if reverse else 0
    halo_per_chunk = CHUNK // HALO_ROWS
    n_halo = rows // HALO_ROWS
    cid = (lambda i, *_: n_chunks - 1 - i) if reverse else (lambda i, *_: i)
    rowblk = lambda i, *_: (cid(i), 0)
    const2 = lambda i, *_: (0, 0)
    once = lambda shape: pl.BlockSpec(shape, const2, pipeline_mode=pl.Buffered(1))
    in_specs = [
        pl.BlockSpec((HALO_ROWS, CONV_DIM), lambda i, *_: (jnp.maximum(cid(i) * halo_per_chunk - 1, 0), 0)),
        pl.BlockSpec((CHUNK, CONV_DIM), rowblk),
        pl.BlockSpec((HALO_ROWS, CONV_DIM),
                     lambda i, *_: (jnp.minimum((cid(i) + 1) * halo_per_chunk, n_halo - 1), 0)),
        pl.BlockSpec((CHUNK, 2 * SSD_HEADS), rowblk),
        pl.BlockSpec((SSD_HEADS, CHUNK), lambda i, *_: (d, cid(i))),
        once((CONV_W, CONV_DIM)), once((1, CONV_DIM)),
        once((1, SSD_HEADS)), once((SSD_HEADS, 1)), once((1, SSD_HEADS)), once((SSD_HEADS, 1)),
    ]
    args = [xbc, xbc, xbc, dt, dtt, conv_w, conv_b[None],
            dt_bias[d][None], dt_bias[d][:, None], a_log[d][None], a_log[d][:, None]]
    resident = 0
    if reverse:
        yf, z, h, d_skip, gate_g, w_out = final
        wob = w_out.astype(jnp.bfloat16)
        in_specs += [
            pl.BlockSpec((CHUNK, D_INNER), rowblk), pl.BlockSpec((CHUNK, D_INNER), rowblk),
            pl.BlockSpec((CHUNK, D_MODEL), rowblk),
            once((1, D_INNER)), once((1, D_INNER)), once(wob.shape),
        ]
        args += [yf, z, h, jnp.repeat(d_skip, SSD_HEAD_DIM)[None], gate_g[None], wob]
        out_spec = pl.BlockSpec((CHUNK, D_MODEL), rowblk)
        out_shape = jax.ShapeDtypeStruct((rows, D_MODEL), jnp.float32)
        resident = wob.size * 2
    else:
        out_spec = pl.BlockSpec((CHUNK, D_INNER), rowblk)
        out_shape = jax.ShapeDtypeStruct((rows, D_INNER), jnp.float32)
    return pl.pallas_call(
        functools.partial(_ssd_scan_kernel, reverse=reverse),
        grid_spec=pltpu.PrefetchScalarGridSpec(
            num_scalar_prefetch=2,
            grid=(n_chunks,),
            in_specs=in_specs,
            out_specs=out_spec,
            scratch_shapes=[
                pltpu.VMEM((CHUNK + 2 * SUBLANES, CONV_DIM), jnp.float32),
                pltpu.VMEM((CHUNK, D_INNER), jnp.float32),
                pltpu.VMEM((SSD_GROUPS, SSD_STATE, CHUNK), jnp.bfloat16),
                pltpu.VMEM((CHUNK, SSD_GROUPS * SSD_STATE), jnp.bfloat16),
                pltpu.VMEM((CHUNK, D_INNER), jnp.float32),
                pltpu.VMEM((SSD_HEADS, SSD_STATE, SSD_HEAD_DIM), jnp.float32),
            ],
        ),
        out_shape=out_shape,
        compiler_params=pltpu.CompilerParams(
            dimension_semantics=("arbitrary",), vmem_limit_bytes=_vmem_limit(resident)),
        name="ssd_scan_bwd" if reverse else "ssd_scan_fwd",
    )(first, last, *args)


def ssd_layer(seq_rows, h, g, w_in, conv_w, conv_b, dt_bias, a_log, d_skip, gate_g, w_out):
    z, xbc, dt, dtt = ssd_in(h, g, w_in)
    yf = ssd_scan(seq_rows, xbc, dt, dtt, conv_w, conv_b, dt_bias, a_log, reverse=False)
    return ssd_scan(seq_rows, xbc, dt, dtt, conv_w, conv_b, dt_bias, a_log, reverse=True,
                    final=(yf, z, h, d_skip, gate_g, w_out))


ATTN_HEAD_DIM = 128
N_Q_HEADS = 8
N_KV_HEADS = 2
KV_REP = N_Q_HEADS // N_KV_HEADS
Q_WIDTH = N_Q_HEADS * ATTN_HEAD_DIM
KV_WIDTH = N_KV_HEADS * ATTN_HEAD_DIM
ATTN_SCALE = ATTN_HEAD_DIM ** -0.5
Q_SCALE_LOG2 = float(ATTN_SCALE * np.log2(np.e))
GRID_W = 64
ROPE_THETA = 10000.0
ROPE_FREQS = ATTN_HEAD_DIM // 4
KEY_BLOCK = 128
Q_ROWS = 128


def rope_tables(seq_rows_max):
    r = np.arange(seq_rows_max)
    tok = r - FRONT
    meta = r - META_PAD
    grow = np.where(tok >= 0, tok // GRID_W, -1).astype(np.float32)
    gcol = np.where(tok >= 0, tok % GRID_W, np.maximum(meta, 0)).astype(np.float32)
    inv_freq = jnp.asarray(ROPE_THETA, jnp.float32) ** (-jnp.arange(ROPE_FREQS, dtype=jnp.float32) / ROPE_FREQS)
    ang = jnp.concatenate([jnp.asarray(grow)[:, None] * inv_freq, jnp.asarray(gcol)[:, None] * inv_freq], axis=-1)
    ang = jnp.concatenate([ang, ang], axis=-1)
    sign = np.where(np.arange(ATTN_HEAD_DIM) < ATTN_HEAD_DIM // 2, -1.0, 1.0).astype(np.float32)
    return jnp.cos(ang), jnp.sin(ang) * sign


def _attn_in_kernel(h_ref, g_ref, wq_ref, wkt_ref, wv_ref, qg_ref, kg_ref, cos_ref, sin_ref, cost_ref, sint_ref,
                    q_ref, kt_ref, v_ref):
    rows = h_ref.shape[0]
    xb = _rms(h_ref[...], g_ref[...]).astype(jnp.bfloat16)
    q = jnp.dot(xb, wq_ref[...], preferred_element_type=jnp.float32)
    v_ref[...] = jnp.dot(xb, wv_ref[...], preferred_element_type=jnp.float32).astype(v_ref.dtype)
    kt = lax.dot_general(wkt_ref[...], xb, (((1,), (1,)), ((), ())), preferred_element_type=jnp.float32)
    cos, sin = cos_ref[...], sin_ref[...]
    half = ATTN_HEAD_DIM // 2
    for hd in range(N_Q_HEADS):
        sl = slice(hd * ATTN_HEAD_DIM, (hd + 1) * ATTN_HEAD_DIM)
        x = _rms(q[:, sl], qg_ref[...])
        rot = jnp.concatenate([x[:, half:], x[:, :half]], axis=-1)
        q_ref[:, sl] = ((x * cos + rot * sin) * Q_SCALE_LOG2).astype(q_ref.dtype)
    cos_t, sin_t = cost_ref[...], sint_ref[...]
    for hd in range(N_KV_HEADS):
        x = kt[hd * ATTN_HEAD_DIM:(hd + 1) * ATTN_HEAD_DIM, :]
        x = x * lax.rsqrt(jnp.mean(x * x, axis=0, keepdims=True) + RMS_EPS) * kg_ref[...]
        rot = jnp.concatenate([x[half:, :], x[:half, :]], axis=0)
        x = (x * cos_t + rot * sin_t).astype(kt_ref.dtype)
        for b in range(rows // KEY_BLOCK):
            kt_ref[b, hd * ATTN_HEAD_DIM:(hd + 1) * ATTN_HEAD_DIM, :] = x[:, b * KEY_BLOCK:(b + 1) * KEY_BLOCK]


def attn_in(h, g, w_qkv, q_g, k_g, cos, sin):
    rows = h.shape[0]
    tm = DENSE_ROWS
    assert rows % tm == 0
    wb = w_qkv.astype(jnp.bfloat16)
    wq, wk, wv = wb[:, :Q_WIDTH], wb[:, Q_WIDTH:Q_WIDTH + KV_WIDTH], wb[:, Q_WIDTH + KV_WIDTH:]
    once = lambda shape: pl.BlockSpec(shape, lambda i: (0,) * len(shape), pipeline_mode=pl.Buffered(1))
    return pl.pallas_call(
        _attn_in_kernel,
        grid=(rows // tm,),
        in_specs=[
            pl.BlockSpec((tm, D_MODEL), lambda i: (i, 0)),
            once((1, D_MODEL)), once(wq.shape), once((KV_WIDTH, D_MODEL)), once(wv.shape),
            once((1, ATTN_HEAD_DIM)), once((ATTN_HEAD_DIM, 1)),
            pl.BlockSpec((tm, ATTN_HEAD_DIM), lambda i: (i, 0)),
            pl.BlockSpec((tm, ATTN_HEAD_DIM), lambda i: (i, 0)),
            pl.BlockSpec((ATTN_HEAD_DIM, tm), lambda i: (0, i)),
            pl.BlockSpec((ATTN_HEAD_DIM, tm), lambda i: (0, i)),
        ],
        out_specs=[
            pl.BlockSpec((tm, Q_WIDTH), lambda i: (i, 0)),
            pl.BlockSpec((tm // KEY_BLOCK, KV_WIDTH, KEY_BLOCK), lambda i: (i, 0, 0)),
            pl.BlockSpec((tm, KV_WIDTH), lambda i: (i, 0)),
        ],
        out_shape=[
            jax.ShapeDtypeStruct((rows, Q_WIDTH), jnp.bfloat16),
            jax.ShapeDtypeStruct((rows // KEY_BLOCK, KV_WIDTH, KEY_BLOCK), jnp.bfloat16),
            jax.ShapeDtypeStruct((rows, KV_WIDTH), jnp.bfloat16),
        ],
        compiler_params=pltpu.CompilerParams(
            dimension_semantics=("arbitrary",), vmem_limit_bytes=_vmem_limit(wb.size * 2)),
        name="attn_in",
    )(h, g, wq, wk.T, wv, q_g[None], k_g[:, None], cos, sin, cos.T, sin.T)


def _attn_kernel(q_ref, kt_ref, v_ref, h_ref, wo_ref, o_ref, o_acc, *, key_blocks):
    n_blocks = kt_ref.shape[0]
    n_steps = (n_blocks - 1) // key_blocks
    stacked = KV_REP * Q_ROWS
    qs = [jnp.concatenate([q_ref[:, (g * KV_REP + r) * ATTN_HEAD_DIM:(g * KV_REP + r + 1) * ATTN_HEAD_DIM]
                           for r in range(KV_REP)], axis=0) for g in range(N_KV_HEADS)]

    def step(block0, blocks, carry, mask_front):
        width = blocks * KEY_BLOCK
        ones_col = jnp.where(lax.broadcasted_iota(jnp.int32, (width, LANES), 1) == 0, 1.0, 0.0
                             ).astype(jnp.bfloat16)
        new = []
        for g in range(N_KV_HEADS):
            m, l, acc = carry[g]
            ks = slice(g * ATTN_HEAD_DIM, (g + 1) * ATTN_HEAD_DIM)
            kt = jnp.concatenate([kt_ref[block0 + b, ks, :] for b in range(blocks)], axis=1)
            s = jnp.dot(qs[g], kt, preferred_element_type=jnp.float32)
            if mask_front:
                s = jnp.where(lax.broadcasted_iota(jnp.int32, (1, width), 1) >= META_PAD, s, NEG_INF)
            m_new = jnp.maximum(m, jnp.max(s, axis=-1, keepdims=True))
            alpha = jnp.exp2(m - m_new)
            p = jnp.exp2(s - m_new).astype(jnp.bfloat16)
            v = v_ref[pl.ds(pl.multiple_of(block0 * KEY_BLOCK, KEY_BLOCK), width), ks]
            r = jnp.dot(p, jnp.concatenate([v, ones_col], axis=1), preferred_element_type=jnp.float32)
            new.append((m_new, alpha * l + r[:, ATTN_HEAD_DIM:ATTN_HEAD_DIM + 1],
                        alpha * acc + r[:, :ATTN_HEAD_DIM]))
        return tuple(new)

    init = tuple((jnp.full((stacked, 1), NEG_INF, jnp.float32), jnp.zeros((stacked, 1), jnp.float32),
                  jnp.zeros((stacked, ATTN_HEAD_DIM), jnp.float32)) for _ in range(N_KV_HEADS))
    carry = step(0, 1, init, True)
    carry = lax.fori_loop(0, n_steps, lambda st, c: step(1 + st * key_blocks, key_blocks, c, False), carry)
    for g in range(N_KV_HEADS):
        _, l, acc = carry[g]
        out = (acc / l).astype(jnp.bfloat16)
        for r in range(KV_REP):
            hd = g * KV_REP + r
            o_acc[:, hd * ATTN_HEAD_DIM:(hd + 1) * ATTN_HEAD_DIM] = out[r * Q_ROWS:(r + 1) * Q_ROWS]
    o_ref[...] = h_ref[...] + jnp.dot(o_acc[...], wo_ref[...], preferred_element_type=jnp.float32)


MAX_KEY_BLOCKS = 8


def _key_blocks_per_step(n_blocks):
    return max(d for d in range(1, MAX_KEY_BLOCKS + 1) if (n_blocks - 1) % d == 0)


def attention(n_seq, q, kt, v, h, w_o):
    rows = q.shape[0]
    seq = rows // n_seq
    n_blocks = seq // KEY_BLOCK
    qb = seq // Q_ROWS
    wob = w_o.astype(jnp.bfloat16)
    kernel_fn = functools.partial(_attn_kernel, key_blocks=_key_blocks_per_step(n_blocks))
    return pl.pallas_call(
        kernel_fn,
        grid=(n_seq, qb),
        in_specs=[
            pl.BlockSpec((Q_ROWS, Q_WIDTH), lambda b, i: (b * qb + i, 0)),
            pl.BlockSpec((n_blocks, KV_WIDTH, KEY_BLOCK), lambda b, i: (b, 0, 0), pipeline_mode=pl.Buffered(1)),
            pl.BlockSpec((seq, KV_WIDTH), lambda b, i: (b, 0), pipeline_mode=pl.Buffered(1)),
            pl.BlockSpec((Q_ROWS, D_MODEL), lambda b, i: (b * qb + i, 0)),
            pl.BlockSpec(wob.shape, lambda b, i: (0, 0), pipeline_mode=pl.Buffered(1)),
        ],
        out_specs=pl.BlockSpec((Q_ROWS, D_MODEL), lambda b, i: (b * qb + i, 0)),
        out_shape=jax.ShapeDtypeStruct((rows, D_MODEL), jnp.float32),
        scratch_shapes=[pltpu.VMEM((Q_ROWS, Q_WIDTH), jnp.bfloat16)],
        compiler_params=pltpu.CompilerParams(
            dimension_semantics=("arbitrary", "arbitrary"),
            vmem_limit_bytes=_vmem_limit(2 * seq * KV_WIDTH * 2 + wob.size * 2)),
        name="attention",
    )(q, kt, v, h, wob)


def _to_rows(x, meta):
    b, s, d = x.shape
    front = jnp.concatenate([jnp.zeros((META_PAD, d), x.dtype), meta.astype(x.dtype)], axis=0)
    rows = jnp.concatenate([jnp.broadcast_to(front[None], (b, FRONT, d)), x], axis=1)
    return rows.reshape(b * (FRONT + s), d)


def kernel(x_prompt, x_sample, meta_tokens, norm_mix_g, norm_ffn_g, ssd_w_in, ssd_conv_w, ssd_conv_b, ssd_dt_bias,
           ssd_a_log, ssd_d_skip, ssd_gate_norm_g, ssd_w_out, attn_w_qkv, attn_q_norm_g, attn_k_norm_g, attn_w_o,
           peer_w_q, peer_sub_keys, peer_u, peer_v):
    trunks = (x_prompt, x_sample)
    seq_rows = [FRONT + x.shape[1] for x in trunks for _ in range(x.shape[0])]
    trunk_rows = [x.shape[0] * (FRONT + x.shape[1]) for x in trunks]
    h = jnp.concatenate([_to_rows(x, meta_tokens) for x in trunks], axis=0)

    cos, sin = rope_tables(max(seq_rows))
    cos_all = jnp.concatenate([cos[:n] for n in seq_rows], axis=0)
    sin_all = jnp.concatenate([sin[:n] for n in seq_rows], axis=0)

    depth = norm_mix_g.shape[0]
    for i in range(depth):
        j = i // 2
        g_mix = norm_mix_g[i][None]
        if i % 2 == 0:
            h = ssd_layer(seq_rows, h, g_mix, ssd_w_in[j], ssd_conv_w[j], ssd_conv_b[j], ssd_dt_bias[j],
                          ssd_a_log[j], ssd_d_skip[j], ssd_gate_norm_g[j], ssd_w_out[j])
        else:
            q, kt, v = attn_in(h, g_mix, attn_w_qkv[j], attn_q_norm_g[j], attn_k_norm_g[j], cos_all, sin_all)
            parts, r0 = [], 0
            for x, n in zip(trunks, trunk_rows):
                r1 = r0 + n
                parts.append(attention(x.shape[0], q[r0:r1], kt[r0 // KEY_BLOCK:r1 // KEY_BLOCK], v[r0:r1],
                                       h[r0:r1], attn_w_o[j]))
                r0 = r1
            h = jnp.concatenate(parts, axis=0)
        wqt = peer_w_q[i].T.astype(jnp.bfloat16)
        keys = peer_sub_keys[i].reshape(2 * PEER_HEADS, N_KEYS, PEER_HALF).astype(jnp.bfloat16)
        h = peer_layer(h, norm_ffn_g[i][None], wqt, keys,
                       pack_expert_table(peer_u[i]), pack_expert_table(peer_v[i]))

    outs, r0 = [], 0
    for x, n in zip(trunks, trunk_rows):
        b, s, d = x.shape
        outs.append(h[r0:r0 + n].reshape(b, FRONT + s, d)[:, FRONT:])
        r0 += n
    return tuple(outs)
```

```python
import functools

import jax
import jax.numpy as jnp
import numpy as np
from jax import lax
from jax.experimental import pallas as pl
from jax.experimental.pallas import tpu as pltpu

LANES = 128
SUBLANES = 8
VREG_ELEMS = LANES * SUBLANES
V7X_VMEM_BYTES = 64 * 1024 * 1024

D_MODEL = 1024
N_META = 16
FRONT = 128
META_PAD = FRONT - N_META
RMS_EPS = 1e-6

PEER_HEADS = 8
N_KEYS = 128
PEER_HALF = 128
PEER_TOPK = 16
N_ASSIGN = PEER_HEADS * PEER_TOPK
N_EXPERTS = N_KEYS * N_KEYS
D_CHUNKS = D_MODEL // LANES
HALF_CHUNKS = D_CHUNKS // 2

NEG_INF = float("-inf")


def _vmem_limit(resident_bytes):
    return int(min(V7X_VMEM_BYTES - 8 * 1024 * 1024, resident_bytes + 20 * 1024 * 1024))


def _rms(x, g):
    return x * lax.rsqrt(jnp.mean(x * x, axis=-1, keepdims=True) + RMS_EPS) * g


PEER_Q_ROWS = 256
TOPK_LOCKSTEP = 4
PAIRS = tuple((i, j) for i in range(PEER_TOPK) for j in range(PEER_TOPK) if (i + 1) * (j + 1) <= PEER_TOPK)
N_PAIRS = len(PAIRS)
PAIR_ROWS = -(-N_PAIRS // SUBLANES) * SUBLANES


def _top16_rows(score_list, n_rows):
    rid = lax.broadcasted_iota(jnp.int32, score_list[0].shape, 0).astype(jnp.float32)
    score_list = list(score_list)
    vals = [[] for _ in score_list]
    ids = [[] for _ in score_list]
    for _ in range(PEER_TOPK):
        for k, s in enumerate(score_list):
            m = jnp.max(s, axis=0, keepdims=True)
            i = jnp.min(jnp.where(s == m, rid, float(n_rows)), axis=0, keepdims=True)
            vals[k].append(m)
            ids[k].append(i)
            score_list[k] = jnp.where(rid == i, NEG_INF, s)
    return [(jnp.concatenate(v, axis=0), jnp.concatenate(i, axis=0)) for v, i in zip(vals, ids)]


def _pair_selectors():
    sel0 = np.zeros((PAIR_ROWS, PEER_TOPK), np.float32)
    sel1 = np.zeros((PAIR_ROWS, PEER_TOPK), np.float32)
    for r, (i, j) in enumerate(PAIRS):
        sel0[r, i] = 1.0
        sel1[r, j] = 1.0
    return jnp.asarray(sel0, jnp.bfloat16), jnp.asarray(sel1, jnp.bfloat16)


def _peer_topk_kernel(h_ref, g_ref, wqt_ref, keys_ref, sel0_ref, sel1_ref, x3_ref, eid_ref, gate_ref):
    rows = h_ref.shape[0]
    x = _rms(h_ref[...], g_ref[...])
    for c in range(D_CHUNKS):
        x3_ref[pl.ds(c, rows, stride=D_CHUNKS), :] = x[:, c * LANES:(c + 1) * LANES]
    xb = x.astype(jnp.bfloat16)
    eids, gates = [], []
    for hd0 in range(0, PEER_HEADS, TOPK_LOCKSTEP):
        heads = range(hd0, hd0 + TOPK_LOCKSTEP)
        scores = []
        for j in range(2 * hd0, 2 * (hd0 + TOPK_LOCKSTEP)):
            qt = lax.dot_general(wqt_ref[j * PEER_HALF:(j + 1) * PEER_HALF, :], xb,
                                 (((1,), (1,)), ((), ())), preferred_element_type=jnp.float32)
            scores.append(jnp.dot(keys_ref[j], qt.astype(jnp.bfloat16), preferred_element_type=jnp.float32))
        lists = _top16_rows(scores, N_KEYS)
        cands, cidxs = [], []
        for k in range(TOPK_LOCKSTEP):
            (sv0, si0), (sv1, si1) = lists[2 * k], lists[2 * k + 1]
            cand = _dot3(sel0_ref[...], sv0, left=True) + _dot3(sel1_ref[...], sv1, left=True)
            pos = lax.broadcasted_iota(jnp.int32, cand.shape, 0).astype(jnp.float32)
            cands.append(jnp.where(pos < N_PAIRS, cand, NEG_INF))
            cidxs.append(
                jnp.dot(sel0_ref[...], si0.astype(jnp.bfloat16), preferred_element_type=jnp.float32) * N_KEYS
                + jnp.dot(sel1_ref[...], si1.astype(jnp.bfloat16), preferred_element_type=jnp.float32))
        tops = [[] for _ in heads]
        sels = [[] for _ in heads]
        for _ in range(PEER_TOPK):
            for k in range(TOPK_LOCKSTEP):
                m = jnp.max(cands[k], axis=0, keepdims=True)
                p = jnp.min(jnp.where(cands[k] == m, pos, float(PAIR_ROWS)), axis=0, keepdims=True)
                hit = pos == p
                tops[k].append(m)
                sels[k].append(jnp.sum(jnp.where(hit, cidxs[k], 0.0), axis=0, keepdims=True))
                cands[k] = jnp.where(hit, NEG_INF, cands[k])
        for k in range(TOPK_LOCKSTEP):
            top = jnp.concatenate(tops[k], axis=0)
            e = jnp.exp(top - top[0:1])
            gates.append(e / jnp.sum(e, axis=0, keepdims=True))
            eids.append(jnp.concatenate(sels[k], axis=0))
    gate_t = jnp.concatenate(gates, axis=0)
    eid_t = jnp.concatenate(eids, axis=0)
    for b in range(rows // LANES):
        sl = slice(b * LANES, (b + 1) * LANES)
        gate_ref[sl, :] = gate_t[:, sl].T
        eid_ref[sl, :] = eid_t[:, sl].T.astype(jnp.int32) * HALF_CHUNKS


def peer_topk(h, g, wqt, keys):
    rows = h.shape[0]
    tb = PEER_Q_ROWS
    assert rows % tb == 0
    const2 = lambda i: (0, 0)
    sel0, sel1 = _pair_selectors()
    return pl.pallas_call(
        _peer_topk_kernel,
        grid=(rows // tb,),
        in_specs=[
            pl.BlockSpec((tb, D_MODEL), lambda i: (i, 0)),
            pl.BlockSpec((1, D_MODEL), const2),
            pl.BlockSpec(wqt.shape, const2, pipeline_mode=pl.Buffered(1)),
            pl.BlockSpec(keys.shape, lambda i: (0, 0, 0), pipeline_mode=pl.Buffered(1)),
            pl.BlockSpec(sel0.shape, const2),
            pl.BlockSpec(sel1.shape, const2),
        ],
        out_specs=[
            pl.BlockSpec((tb * D_CHUNKS, LANES), lambda i: (i, 0)),
            pl.BlockSpec((tb, N_ASSIGN), lambda i: (i, 0)),
            pl.BlockSpec((tb, N_ASSIGN), lambda i: (i, 0)),
        ],
        out_shape=[
            jax.ShapeDtypeStruct((rows * D_CHUNKS, LANES), jnp.float32),
            jax.ShapeDtypeStruct((rows, N_ASSIGN), jnp.int32),
            jax.ShapeDtypeStruct((rows, N_ASSIGN), jnp.float32),
        ],
        compiler_params=pltpu.CompilerParams(
            dimension_semantics=("arbitrary",),
            vmem_limit_bytes=_vmem_limit(wqt.size * 2 + keys.size * 2)),
        name="peer_topk",
    )(h, g, wqt, keys, sel0, sel1)


PEER_ROWS = 128
HIGH_HALF = 0xFFFF0000


def _unpack_pair(words):
    return (lax.bitcast_convert_type(words << 16, jnp.float32),
            lax.bitcast_convert_type(words & jnp.uint32(HIGH_HALF), jnp.float32))


def pack_expert_table(w):
    b = lax.bitcast_convert_type(w.astype(jnp.bfloat16), jnp.uint16).astype(jnp.uint32)
    b = b.reshape(N_EXPERTS, 2, HALF_CHUNKS, LANES)
    return (b[:, 0] | (b[:, 1] << 16)).reshape(N_EXPERTS * HALF_CHUNKS, LANES)


def _split3(x):
    hi = x.astype(jnp.bfloat16)
    r = x - hi.astype(jnp.float32)
    mid = r.astype(jnp.bfloat16)
    lo = (r - mid.astype(jnp.float32)).astype(jnp.bfloat16)
    return hi, mid, lo


def _dot3(x, m, left=False):
    if left:
        return sum(jnp.dot(x, p, preferred_element_type=jnp.float32) for p in _split3(m))
    return sum(jnp.dot(p, m, preferred_element_type=jnp.float32) for p in _split3(x))


ACT_GROUP = 4
PAIR_CHUNK_OF_ROW = tuple((q % 2) * HALF_CHUNKS + q // 2 for q in range(D_CHUNKS))


def _gather_tile(eid_ref, t, tbl_ref, tile_ref):
    for a in range(N_ASSIGN):
        e4 = pl.multiple_of(eid_ref[t, a], HALF_CHUNKS)
        tile_ref[a * HALF_CHUNKS:(a + 1) * HALF_CHUNKS, :] = tbl_ref[pl.ds(e4, HALF_CHUNKS), :]


def _for_each_token(rows, gather, compute, tiles_a, tiles_b):
    group = len(tiles_a)
    n_groups = rows // group

    def step(j_compute, tiles_compute, j_gather, tiles_gather):
        for k in range(group):
            compute(j_compute * group + k, tiles_compute[k])
            gather(j_gather * group + k, tiles_gather[k])

    for k, tile in enumerate(tiles_a):
        gather(k, tile)

    def pair(i, carry):
        j = 2 * i
        step(j, tiles_a, j + 1, tiles_b)

        @pl.when(j + 1 < n_groups)
        def _():
            step(j + 1, tiles_b, jnp.minimum(j + 2, n_groups - 1), tiles_a)

        return carry

    lax.fori_loop(0, n_groups // 2, pair, 0)


def _peer_act_kernel(eid_ref, x3_ref, gate_ref, group_ref, tbl_ref, w_ref, part_ref, *tiles):
    rows = gate_ref.shape[0]
    q = lax.broadcasted_iota(jnp.int32, (SUBLANES, N_ASSIGN * SUBLANES), 0)
    n = lax.broadcasted_iota(jnp.int32, (SUBLANES, N_ASSIGN * SUBLANES), 1)
    mask = (n % SUBLANES) == q
    zeros = jnp.zeros((SUBLANES, LANES), jnp.bfloat16)

    def compute(t, tile_ref):
        xt = x3_ref[pl.ds(pl.multiple_of(t * D_CHUNKS, D_CHUNKS), D_CHUNKS), :]
        xp = jnp.concatenate([xt[c:c + 1] for c in PAIR_CHUNK_OF_ROW], axis=0).astype(jnp.bfloat16)
        lhs = jnp.concatenate([xp, zeros], axis=0)
        u = pltpu.bitcast(tile_ref[...], jnp.bfloat16)
        r = lax.dot_general(lhs, u, (((1,), (1,)), ((), ())), preferred_element_type=jnp.float32)
        part_ref[pl.ds(t, 1), :] = jnp.sum(jnp.where(mask, r[0:SUBLANES], 0.0), axis=0, keepdims=True)

    gather = lambda t, tile_ref: _gather_tile(eid_ref, t, tbl_ref, tile_ref)
    _for_each_token(rows, gather, compute, tiles[:ACT_GROUP], tiles[ACT_GROUP:])
    act = _dot3(part_ref[...], group_ref[...])
    gelu = 0.5 * act * (1.0 + lax.erf(act * np.float32(1.0 / np.sqrt(2.0))))
    w_ref[...] = gate_ref[...] * gelu


def _peer_out_kernel(eid_ref, w_ref, h_ref, tbl_ref, o_ref, wrep_ref, acc_ref):
    rows = w_ref.shape[0]

    def expand(t):
        return jnp.broadcast_to(w_ref[pl.ds(t, 1), :], (LANES, N_ASSIGN)).T

    def token(t, wrep):
        wrep_ref[...] = wrep
        wrep_next = expand(jnp.minimum(t + 1, rows - 1))
        lo = [jnp.zeros((HALF_CHUNKS, LANES), jnp.float32) for _ in range(2)]
        hi = [jnp.zeros((HALF_CHUNKS, LANES), jnp.float32) for _ in range(2)]
        for a in range(N_ASSIGN):
            e4 = pl.multiple_of(eid_ref[t, a], HALF_CHUNKS)
            row_lo, row_hi = _unpack_pair(tbl_ref[pl.ds(e4, HALF_CHUNKS), :])
            wv = wrep_ref[pl.ds(a, 1), :]
            lo[a % 2] = lo[a % 2] + wv * row_lo
            hi[a % 2] = hi[a % 2] + wv * row_hi
        base = pl.multiple_of(t * D_CHUNKS, D_CHUNKS)
        acc_ref[pl.ds(base, HALF_CHUNKS), :] = lo[0] + lo[1]
        acc_ref[pl.ds(base + HALF_CHUNKS, HALF_CHUNKS), :] = hi[0] + hi[1]
        return wrep_next

    lax.fori_loop(0, rows, token, expand(0))
    for c in range(D_CHUNKS):
        sl = slice(c * LANES, (c + 1) * LANES)
        o_ref[:, sl] = h_ref[:, sl] + acc_ref[pl.ds(c, rows, stride=D_CHUNKS), :]


def _group_matrix():
    n = np.arange(N_ASSIGN * SUBLANES)
    return jnp.asarray((n[:, None] // SUBLANES) == np.arange(N_ASSIGN)[None, :], jnp.bfloat16)


def peer_act(eid, x3, gate, tbl):
    rows = eid.shape[0]
    tb = PEER_ROWS
    assert rows % tb == 0
    group = _group_matrix()
    return pl.pallas_call(
        _peer_act_kernel,
        grid=(rows // tb,),
        in_specs=[
            pl.BlockSpec((tb, N_ASSIGN), lambda i: (i, 0), memory_space=pltpu.SMEM),
            pl.BlockSpec((tb * D_CHUNKS, LANES), lambda i: (i, 0)),
            pl.BlockSpec((tb, N_ASSIGN), lambda i: (i, 0)),
            pl.BlockSpec(group.shape, lambda i: (0, 0)),
            pl.BlockSpec(tbl.shape, lambda i: (0, 0), pipeline_mode=pl.Buffered(1)),
        ],
        out_specs=pl.BlockSpec((tb, N_ASSIGN), lambda i: (i, 0)),
        out_shape=jax.ShapeDtypeStruct((rows, N_ASSIGN), jnp.float32),
        scratch_shapes=[pltpu.VMEM((tb, N_ASSIGN * SUBLANES), jnp.float32)] + [
            pltpu.VMEM((N_ASSIGN * HALF_CHUNKS, LANES), jnp.uint32) for _ in range(2 * ACT_GROUP)],
        compiler_params=pltpu.CompilerParams(
            dimension_semantics=("arbitrary",), vmem_limit_bytes=_vmem_limit(tbl.size * 4)),
        name="peer_act",
    )(eid, x3, gate, group, tbl)


def peer_out(eid, w, h, tbl):
    rows = eid.shape[0]
    tb = PEER_ROWS
    assert rows % tb == 0
    return pl.pallas_call(
        _peer_out_kernel,
        grid=(rows // tb,),
        in_specs=[
            pl.BlockSpec((tb, N_ASSIGN), lambda i: (i, 0), memory_space=pltpu.SMEM),
            pl.BlockSpec((tb, N_ASSIGN), lambda i: (i, 0)),
            pl.BlockSpec((tb, D_MODEL), lambda i: (i, 0)),
            pl.BlockSpec(tbl.shape, lambda i: (0, 0), pipeline_mode=pl.Buffered(1)),
        ],
        out_specs=pl.BlockSpec((tb, D_MODEL), lambda i: (i, 0)),
        out_shape=jax.ShapeDtypeStruct((rows, D_MODEL), jnp.float32),
        scratch_shapes=[
            pltpu.VMEM((N_ASSIGN, LANES), jnp.float32),
            pltpu.VMEM((tb * D_CHUNKS, LANES), jnp.float32),
        ],
        compiler_params=pltpu.CompilerParams(
            dimension_semantics=("arbitrary",), vmem_limit_bytes=_vmem_limit(tbl.size * 4)),
        name="peer_out",
    )(eid, w, h, tbl)


def peer_layer(h, g, wqt, keys, u_tbl, v_tbl):
    x3, eid, gate = peer_topk(h, g, wqt, keys)
    w = peer_act(eid, x3, gate, u_tbl)
    return peer_out(eid, w, h, v_tbl)


D_INNER = 2048
SSD_HEAD_DIM = 64
SSD_HEADS = D_INNER // SSD_HEAD_DIM
SSD_GROUPS = 4
HEADS_PER_GROUP = SSD_HEADS // SSD_GROUPS
SSD_STATE = 128
CONV_W = 5
CONV_HALF = (CONV_W - 1) // 2
CONV_DIM = D_INNER + 2 * SSD_GROUPS * SSD_STATE
CHUNK = 128
HALO_ROWS = 16
DENSE_ROWS = 256


def _ssd_in_kernel(h_ref, g_ref, wz_ref, wx_ref, wdt_ref, wdtt_ref, z_ref, xbc_ref, dt_ref, dtt_ref):
    xb = _rms(h_ref[...], g_ref[...]).astype(jnp.bfloat16)
    z_ref[...] = jnp.dot(xb, wz_ref[...], preferred_element_type=jnp.float32).astype(z_ref.dtype)
    xbc_ref[...] = jnp.dot(xb, wx_ref[...], preferred_element_type=jnp.float32).astype(xbc_ref.dtype)
    dt_ref[...] = jnp.dot(xb, wdt_ref[...], preferred_element_type=jnp.float32)
    dtt_ref[...] = lax.dot_general(wdtt_ref[...], xb, (((1,), (1,)), ((), ())),
                                   preferred_element_type=jnp.float32)


def ssd_in(h, g, w_in):
    rows = h.shape[0]
    tm = DENSE_ROWS
    assert rows % tm == 0
    wb = w_in.astype(jnp.bfloat16)
    wz, wx, wdt = wb[:, :D_INNER], wb[:, D_INNER:D_INNER + CONV_DIM], wb[:, D_INNER + CONV_DIM:]
    n_dt = wdt.shape[1]
    once = lambda shape: pl.BlockSpec(shape, lambda i: (0,) * len(shape), pipeline_mode=pl.Buffered(1))
    return pl.pallas_call(
        _ssd_in_kernel,
        grid=(rows // tm,),
        in_specs=[
            pl.BlockSpec((tm, D_MODEL), lambda i: (i, 0)),
            once((1, D_MODEL)), once(wz.shape), once(wx.shape), once(wdt.shape), once((n_dt, D_MODEL)),
        ],
        out_specs=[
            pl.BlockSpec((tm, D_INNER), lambda i: (i, 0)),
            pl.BlockSpec((tm, CONV_DIM), lambda i: (i, 0)),
            pl.BlockSpec((tm, n_dt), lambda i: (i, 0)),
            pl.BlockSpec((n_dt, tm), lambda i: (0, i)),
        ],
        out_shape=[
            jax.ShapeDtypeStruct((rows, D_INNER), jnp.bfloat16),
            jax.ShapeDtypeStruct((rows, CONV_DIM), jnp.bfloat16),
            jax.ShapeDtypeStruct((rows, n_dt), jnp.float32),
            jax.ShapeDtypeStruct((n_dt, rows), jnp.float32),
        ],
        compiler_params=pltpu.CompilerParams(
            dimension_semantics=("arbitrary",), vmem_limit_bytes=_vmem_limit(wb.size * 2)),
        name="ssd_in",
    )(h, g, wz, wx, wdt, wdt.T)


def _dot2(x, m):
    hi = x.astype(jnp.bfloat16)
    lo = (x - hi.astype(jnp.float32)).astype(jnp.bfloat16)
    return (jnp.dot(hi, m, preferred_element_type=jnp.float32)
            + jnp.dot(lo, m, preferred_element_type=jnp.float32))


def _softplus(x):
    return jnp.maximum(x, 0.0) + jnp.log(1.0 + jnp.exp(-jnp.abs(x)))


def _silu(x):
    return x / (1.0 + jnp.exp(-x))


def _ssd_scan_kernel(first_ref, last_ref,
                     prev_ref, cur_ref, next_ref, dt_ref, dtt_ref,
                     convw_ref, convb_ref, bias_ref, biast_ref, alog_ref, alogt_ref, *rest,
                     reverse):
    if reverse:
        (yf_ref, z_ref, h_ref, dskip_ref, gn_ref, wout_ref, o_ref,
         ext_ref, xs_ref, bt_ref, c_ref, y_ref, state_ref) = rest
    else:
        o_ref, ext_ref, xs_ref, bt_ref, c_ref, y_ref, state_ref = rest
    ci = pl.program_id(0)
    n_chunks = pl.num_programs(0)
    chunk_id = (n_chunks - 1 - ci) if reverse else ci
    seq_first = first_ref[chunk_id] == 1
    seq_last = last_ref[chunk_id] == 1
    starts = seq_last if reverse else seq_first

    @pl.when(starts)
    def _():
        state_ref[...] = jnp.zeros_like(state_ref)

    keep_prev = jnp.where(seq_first, 0.0, 1.0)
    keep_next = jnp.where(seq_last, 0.0, 1.0)
    ext_ref[0:SUBLANES, :] = prev_ref[HALO_ROWS - SUBLANES:, :].astype(jnp.float32) * keep_prev
    ext_ref[SUBLANES:SUBLANES + CHUNK, :] = cur_ref[...].astype(jnp.float32)
    ext_ref[SUBLANES + CHUNK:, :] = next_ref[0:SUBLANES, :].astype(jnp.float32) * keep_next
    strip = 4 * LANES
    for s0 in range(0, CONV_DIM, strip):
        acc = jnp.zeros((CHUNK, strip), jnp.float32) + convb_ref[:, s0:s0 + strip]
        for k in range(CONV_W):
            r0 = SUBLANES - CONV_HALF + k
            acc = acc + ext_ref[r0:r0 + CHUNK, s0:s0 + strip] * convw_ref[k:k + 1, s0:s0 + strip]
        act = _silu(acc)
        if s0 < D_INNER:
            xs_ref[:, s0:s0 + strip] = act
        elif s0 < D_INNER + SSD_GROUPS * SSD_STATE:
            for g in range(strip // SSD_STATE):
                gi = (s0 - D_INNER) // SSD_STATE + g
                bt_ref[gi] = act[:, g * SSD_STATE:(g + 1) * SSD_STATE].T.astype(jnp.bfloat16)
        else:
            b0 = s0 - D_INNER - SSD_GROUPS * SSD_STATE
            c_ref[:, b0:b0 + strip] = act.astype(jnp.bfloat16)

    d0 = SSD_HEADS if reverse else 0
    row = lax.broadcasted_iota(jnp.int32, (CHUNK, SSD_HEADS), 0)
    col = lax.broadcasted_iota(jnp.int32, (SSD_HEADS, CHUNK), 1)
    valid = jnp.logical_or(jnp.logical_not(seq_first), row >= META_PAD)
    valid_t = jnp.logical_or(jnp.logical_not(seq_first), col >= META_PAD)
    dt = jnp.where(valid, _softplus(dt_ref[:, d0:d0 + SSD_HEADS] + bias_ref[...]), 0.0)
    dt_t = jnp.where(valid_t, _softplus(dtt_ref[...] + biast_ref[...]), 0.0)
    adt = dt * -jnp.exp(alog_ref[...])
    adt_t = dt_t * -jnp.exp(alogt_ref[...])
    li = lax.broadcasted_iota(jnp.int32, (CHUNK, CHUNK), 0)
    si = lax.broadcasted_iota(jnp.int32, (CHUNK, CHUNK), 1)
    causal = (li <= si) if reverse else (li >= si)
    tri = jnp.where(causal, 1.0, 0.0).astype(jnp.bfloat16)
    tri_t = jnp.where((li >= si) if reverse else (li <= si), 1.0, 0.0).astype(jnp.bfloat16)
    cs = sum(jnp.dot(tri, p, preferred_element_type=jnp.float32) for p in _split3(adt))
    cs_t = sum(jnp.dot(p, tri_t, preferred_element_type=jnp.float32) for p in _split3(adt_t))
    edge = 0 if reverse else CHUNK - 1
    total = cs[edge:edge + 1, :]
    total_t = cs_t[:, edge:edge + 1]
    hp = lax.broadcasted_iota(jnp.int32, (SSD_HEADS, D_INNER), 0)
    hc = lax.broadcasted_iota(jnp.int32, (SSD_HEADS, D_INNER), 1) // SSD_HEAD_DIM
    expand = jnp.where(hp == hc, 1.0, 0.0).astype(jnp.bfloat16)
    xdt = xs_ref[...] * _dot2(dt, expand)
    xdte = (xdt * _dot2(jnp.exp(total - cs), expand)).astype(jnp.bfloat16)
    xdt = xdt.astype(jnp.bfloat16)
    ecs = _dot2(jnp.exp(cs), expand)
    chunk_decay = jnp.exp(total_t)

    for g in range(SSD_GROUPS):
        b_t = bt_ref[g]
        c_g = c_ref[:, g * SSD_STATE:(g + 1) * SSD_STATE]
        cb = jnp.dot(c_g, b_t, preferred_element_type=jnp.float32)
        for hl in range(HEADS_PER_GROUP):
            hh = g * HEADS_PER_GROUP + hl
            ps = slice(hh * SSD_HEAD_DIM, (hh + 1) * SSD_HEAD_DIM)
            seg = cs[:, hh:hh + 1] - cs_t[hh:hh + 1, :]
            w = jnp.where(causal, cb * jnp.exp(jnp.where(causal, seg, 0.0)), 0.0).astype(jnp.bfloat16)
            state = state_ref[hh]
            y = jnp.dot(w, xdt[:, ps], preferred_element_type=jnp.float32)
            y = y + jnp.dot(c_g, state.astype(jnp.bfloat16), preferred_element_type=jnp.float32) * ecs[:, ps]
            y_ref[:, ps] = y
            state_ref[hh] = state * chunk_decay[hh:hh + 1, :] + jnp.dot(
                b_t, xdte[:, ps], preferred_element_type=jnp.float32)

    if not reverse:
        o_ref[...] = y_ref[...].astype(o_ref.dtype)
        return
    y = y_ref[...] + yf_ref[...].astype(jnp.float32) + dskip_ref[...] * xs_ref[...]
    gated = y * _silu(z_ref[...].astype(jnp.float32))
    normed = _rms(gated, gn_ref[...]).astype(jnp.bfloat16)
    out = jnp.dot(normed, wout_ref[...], preferred_element_type=jnp.float32)
    rowd = lax.broadcasted_iota(jnp.int32, out.shape, 0)
    keep = jnp.logical_or(jnp.logical_not(seq_first), rowd >= META_PAD)
    o_ref[...] = jnp.where(keep, h_ref[...] + out, 0.0)


def _chunk_flags(seq_rows):
    first, last = [], []
    for n in seq_rows:
        assert n % CHUNK == 0
        c = n // CHUNK
        first += [1] + [0] * (c - 1)
        last += [0] * (c - 1) + [1]
    return jnp.asarray(first, jnp.int32), jnp.asarray(last, jnp.int32)


def ssd_scan(seq_rows, xbc, dt, dtt, conv_w, conv_b, dt_bias, a_log, reverse, final=None):
    rows = xbc.shape[0]
    n_chunks = rows // CHUNK
    first, last = _chunk_flags(seq_rows)
    d = 1 if reverse else 0
    halo_per_chunk = CHUNK // HALO_ROWS
    n_halo = rows // HALO_ROWS
    cid = (lambda i, *_: n_chunks - 1 - i) if reverse else (lambda i, *_: i)
    rowblk = lambda i, *_: (cid(i), 0)
    const2 = lambda i, *_: (0, 0)
    once = lambda shape: pl.BlockSpec(shape, const2, pipeline_mode=pl.Buffered(1))
    in_specs = [
        pl.BlockSpec((HALO_ROWS, CONV_DIM), lambda i, *_: (jnp.maximum(cid(i) * halo_per_chunk - 1, 0), 0)),
        pl.BlockSpec((CHUNK, CONV_DIM), rowblk),
        pl.BlockSpec((HALO_ROWS, CONV_DIM),
                     lambda i, *_: (jnp.minimum((cid(i) + 1) * halo_per_chunk, n_halo - 1), 0)),
        pl.BlockSpec((CHUNK, 2 * SSD_HEADS), rowblk),
        pl.BlockSpec((SSD_HEADS, CHUNK), lambda i, *_: (d, cid(i))),
        once((CONV_W, CONV_DIM)), once((1, CONV_DIM)),
        once((1, SSD_HEADS)), once((SSD_HEADS, 1)), once((1, SSD_HEADS)), once((SSD_HEADS, 1)),
    ]
    args = [xbc, xbc, xbc, dt, dtt, conv_w, conv_b[None],
            dt_bias[d][None], dt_bias[d][:, None], a_log[d][None], a_log[d][:, None]]
    resident = 0
    if reverse:
        yf, z, h, d_skip, gate_g, w_out = final
        wob = w_out.astype(jnp.bfloat16)
        in_specs += [
            pl.BlockSpec((CHUNK, D_INNER), rowblk), pl.BlockSpec((CHUNK, D_INNER), rowblk),
            pl.BlockSpec((CHUNK, D_MODEL), rowblk),
            once((1, D_INNER)), once((1, D_INNER)), once(wob.shape),
        ]
        args += [yf, z, h, jnp.repeat(d_skip, SSD_HEAD_DIM)[None], gate_g[None], wob]
        out_spec = pl.BlockSpec((CHUNK, D_MODEL), rowblk)
        out_shape = jax.ShapeDtypeStruct((rows, D_MODEL), jnp.float32)
        resident = wob.size * 2
    else:
        out_spec = pl.BlockSpec((CHUNK, D_INNER), rowblk)
        out_shape = jax.ShapeDtypeStruct((rows, D_INNER), jnp.float32)
    return pl.pallas_call(
        functools.partial(_ssd_scan_kernel, reverse=reverse),
        grid_spec=pltpu.PrefetchScalarGridSpec(
            num_scalar_prefetch=2,
            grid=(n_chunks,),
            in_specs=in_specs,
            out_specs=out_spec,
            scratch_shapes=[
                pltpu.VMEM((CHUNK + 2 * SUBLANES, CONV_DIM), jnp.float32),
                pltpu.VMEM((CHUNK, D_INNER), jnp.float32),
                pltpu.VMEM((SSD_GROUPS, SSD_STATE, CHUNK), jnp.bfloat16),
                pltpu.VMEM((CHUNK, SSD_GROUPS * SSD_STATE), jnp.bfloat16),
                pltpu.VMEM((CHUNK, D_INNER), jnp.float32),
                pltpu.VMEM((SSD_HEADS, SSD_STATE, SSD_HEAD_DIM), jnp.float32),
            ],
        ),
        out_shape=out_shape,
        compiler_params=pltpu.CompilerParams(
            dimension_semantics=("arbitrary",), vmem_limit_bytes=_vmem_limit(resident)),
        name="ssd_scan_bwd" if reverse else "ssd_scan_fwd",
    )(first, last, *args)


def ssd_layer(seq_rows, h, g, w_in, conv_w, conv_b, dt_bias, a_log, d_skip, gate_g, w_out):
    z, xbc, dt, dtt = ssd_in(h, g, w_in)
    yf = ssd_scan(seq_rows, xbc, dt, dtt, conv_w, conv_b, dt_bias, a_log, reverse=False)
    return ssd_scan(seq_rows, xbc, dt, dtt, conv_w, conv_b, dt_bias, a_log, reverse=True,
                    final=(yf, z, h, d_skip, gate_g, w_out))


ATTN_HEAD_DIM = 128
N_Q_HEADS = 8
N_KV_HEADS = 2
KV_REP = N_Q_HEADS // N_KV_HEADS
Q_WIDTH = N_Q_HEADS * ATTN_HEAD_DIM
KV_WIDTH = N_KV_HEADS * ATTN_HEAD_DIM
ATTN_SCALE = ATTN_HEAD_DIM ** -0.5
Q_SCALE_LOG2 = float(ATTN_SCALE * np.log2(np.e))
GRID_W = 64
ROPE_THETA = 10000.0
ROPE_FREQS = ATTN_HEAD_DIM // 4
KEY_BLOCK = 128
Q_ROWS = 128


def rope_tables(seq_rows_max):
    r = np.arange(seq_rows_max)
    tok = r - FRONT
    meta = r - META_PAD
    grow = np.where(tok >= 0, tok // GRID_W, -1).astype(np.float32)
    gcol = np.where(tok >= 0, tok % GRID_W, np.maximum(meta, 0)).astype(np.float32)
    inv_freq = jnp.asarray(ROPE_THETA, jnp.float32) ** (-jnp.arange(ROPE_FREQS, dtype=jnp.float32) / ROPE_FREQS)
    ang = jnp.concatenate([jnp.asarray(grow)[:, None] * inv_freq, jnp.asarray(gcol)[:, None] * inv_freq], axis=-1)
    ang = jnp.concatenate([ang, ang], axis=-1)
    sign = np.where(np.arange(ATTN_HEAD_DIM) < ATTN_HEAD_DIM // 2, -1.0, 1.0).astype(np.float32)
    return jnp.cos(ang), jnp.sin(ang) * sign


def _attn_in_kernel(h_ref, g_ref, wq_ref, wkt_ref, wv_ref, qg_ref, kg_ref, cos_ref, sin_ref, cost_ref, sint_ref,
                    q_ref, kt_ref, v_ref):
    rows = h_ref.shape[0]
    xb = _rms(h_ref[...], g_ref[...]).astype(jnp.bfloat16)
    q = jnp.dot(xb, wq_ref[...], preferred_element_type=jnp.float32)
    v_ref[...] = jnp.dot(xb, wv_ref[...], preferred_element_type=jnp.float32).astype(v_ref.dtype)
    kt = lax.dot_general(wkt_ref[...], xb, (((1,), (1,)), ((), ())), preferred_element_type=jnp.float32)
    cos, sin = cos_ref[...], sin_ref[...]
    half = ATTN_HEAD_DIM // 2
    for hd in range(N_Q_HEADS):
        sl = slice(hd * ATTN_HEAD_DIM, (hd + 1) * ATTN_HEAD_DIM)
        x = _rms(q[:, sl], qg_ref[...])
        rot = jnp.concatenate([x[:, half:], x[:, :half]], axis=-1)
        q_ref[:, sl] = ((x * cos + rot * sin) * Q_SCALE_LOG2).astype(q_ref.dtype)
    cos_t, sin_t = cost_ref[...], sint_ref[...]
    for hd in range(N_KV_HEADS):
        x = kt[hd * ATTN_HEAD_DIM:(hd + 1) * ATTN_HEAD_DIM, :]
        x = x * lax.rsqrt(jnp.mean(x * x, axis=0, keepdims=True) + RMS_EPS) * kg_ref[...]
        rot = jnp.concatenate([x[half:, :], x[:half, :]], axis=0)
        x = (x * cos_t + rot * sin_t).astype(kt_ref.dtype)
        for b in range(rows // KEY_BLOCK):
            kt_ref[b, hd * ATTN_HEAD_DIM:(hd + 1) * ATTN_HEAD_DIM, :] = x[:, b * KEY_BLOCK:(b + 1) * KEY_BLOCK]


def attn_in(h, g, w_qkv, q_g, k_g, cos, sin):
    rows = h.shape[0]
    tm = DENSE_ROWS
    assert rows % tm == 0
    wb = w_qkv.astype(jnp.bfloat16)
    wq, wk, wv = wb[:, :Q_WIDTH], wb[:, Q_WIDTH:Q_WIDTH + KV_WIDTH], wb[:, Q_WIDTH + KV_WIDTH:]
    once = lambda shape: pl.BlockSpec(shape, lambda i: (0,) * len(shape), pipeline_mode=pl.Buffered(1))
    return pl.pallas_call(
        _attn_in_kernel,
        grid=(rows // tm,),
        in_specs=[
            pl.BlockSpec((tm, D_MODEL), lambda i: (i, 0)),
            once((1, D_MODEL)), once(wq.shape), once((KV_WIDTH, D_MODEL)), once(wv.shape),
            once((1, ATTN_HEAD_DIM)), once((ATTN_HEAD_DIM, 1)),
            pl.BlockSpec((tm, ATTN_HEAD_DIM), lambda i: (i, 0)),
            pl.BlockSpec((tm, ATTN_HEAD_DIM), lambda i: (i, 0)),
            pl.BlockSpec((ATTN_HEAD_DIM, tm), lambda i: (0, i)),
            pl.BlockSpec((ATTN_HEAD_DIM, tm), lambda i: (0, i)),
        ],
        out_specs=[
            pl.BlockSpec((tm, Q_WIDTH), lambda i: (i, 0)),
            pl.BlockSpec((tm // KEY_BLOCK, KV_WIDTH, KEY_BLOCK), lambda i: (i, 0, 0)),
            pl.BlockSpec((tm, KV_WIDTH), lambda i: (i, 0)),
        ],
        out_shape=[
            jax.ShapeDtypeStruct((rows, Q_WIDTH), jnp.bfloat16),
            jax.ShapeDtypeStruct((rows // KEY_BLOCK, KV_WIDTH, KEY_BLOCK), jnp.bfloat16),
            jax.ShapeDtypeStruct((rows, KV_WIDTH), jnp.bfloat16),
        ],
        compiler_params=pltpu.CompilerParams(
            dimension_semantics=("arbitrary",), vmem_limit_bytes=_vmem_limit(wb.size * 2)),
        name="attn_in",
    )(h, g, wq, wk.T, wv, q_g[None], k_g[:, None], cos, sin, cos.T, sin.T)


def _attn_kernel(q_ref, kt_ref, v_ref, h_ref, wo_ref, o_ref, o_acc, *, key_blocks):
    n_blocks = kt_ref.shape[0]
    n_steps = (n_blocks - 1) // key_blocks
    stacked = KV_REP * Q_ROWS
    qs = [jnp.concatenate([q_ref[:, (g * KV_REP + r) * ATTN_HEAD_DIM:(g * KV_REP + r + 1) * ATTN_HEAD_DIM]
                           for r in range(KV_REP)], axis=0) for g in range(N_KV_HEADS)]

    def step(block0, blocks, carry, mask_front):
        width = blocks * KEY_BLOCK
        ones_col = jnp.where(lax.broadcasted_iota(jnp.int32, (width, LANES), 1) == 0, 1.0, 0.0
                             ).astype(jnp.bfloat16)
        new = []
        for g in range(N_KV_HEADS):
            m, l, acc = carry[g]
            ks = slice(g * ATTN_HEAD_DIM, (g + 1) * ATTN_HEAD_DIM)
            kt = jnp.concatenate([kt_ref[block0 + b, ks, :] for b in range(blocks)], axis=1)
            s = jnp.dot(qs[g], kt, preferred_element_type=jnp.float32)
            if mask_front:
                s = jnp.where(lax.broadcasted_iota(jnp.int32, (1, width), 1) >= META_PAD, s, NEG_INF)
            m_new = jnp.maximum(m, jnp.max(s, axis=-1, keepdims=True))
            alpha = jnp.exp2(m - m_new)
            p = jnp.exp2(s - m_new).astype(jnp.bfloat16)
            v = v_ref[pl.ds(pl.multiple_of(block0 * KEY_BLOCK, KEY_BLOCK), width), ks]
            r = jnp.dot(p, jnp.concatenate([v, ones_col], axis=1), preferred_element_type=jnp.float32)
            new.append((m_new, alpha * l + r[:, ATTN_HEAD_DIM:ATTN_HEAD_DIM + 1],
                        alpha * acc + r[:, :ATTN_HEAD_DIM]))
        return tuple(new)

    init = tuple((jnp.full((stacked, 1), NEG_INF, jnp.float32), jnp.zeros((stacked, 1), jnp.float32),
                  jnp.zeros((stacked, ATTN_HEAD_DIM), jnp.float32)) for _ in range(N_KV_HEADS))
    carry = step(0, 1, init, True)
    carry = lax.fori_loop(0, n_steps, lambda st, c: step(1 + st * key_blocks, key_blocks, c, False), carry)
    for g in range(N_KV_HEADS):
        _, l, acc = carry[g]
        out = (acc / l).astype(jnp.bfloat16)
        for r in range(KV_REP):
            hd = g * KV_REP + r
            o_acc[:, hd * ATTN_HEAD_DIM:(hd + 1) * ATTN_HEAD_DIM] = out[r * Q_ROWS:(r + 1) * Q_ROWS]
    o_ref[...] = h_ref[...] + jnp.dot(o_acc[...], wo_ref[...], preferred_element_type=jnp.float32)


MAX_KEY_BLOCKS = 16


def _key_blocks_per_step(n_blocks):
    return max(d for d in range(1, MAX_KEY_BLOCKS + 1) if (n_blocks - 1) % d == 0)


def attention(n_seq, q, kt, v, h, w_o):
    rows = q.shape[0]
    seq = rows // n_seq
    n_blocks = seq // KEY_BLOCK
    qb = seq // Q_ROWS
    wob = w_o.astype(jnp.bfloat16)
    kernel_fn = functools.partial(_attn_kernel, key_blocks=_key_blocks_per_step(n_blocks))
    return pl.pallas_call(
        kernel_fn,
        grid=(n_seq, qb),
        in_specs=[
            pl.BlockSpec((Q_ROWS, Q_WIDTH), lambda b, i: (b * qb + i, 0)),
            pl.BlockSpec((n_blocks, KV_WIDTH, KEY_BLOCK), lambda b, i: (b, 0, 0), pipeline_mode=pl.Buffered(1)),
            pl.BlockSpec((seq, KV_WIDTH), lambda b, i: (b, 0), pipeline_mode=pl.Buffered(1)),
            pl.BlockSpec((Q_ROWS, D_MODEL), lambda b, i: (b * qb + i, 0)),
            pl.BlockSpec(wob.shape, lambda b, i: (0, 0), pipeline_mode=pl.Buffered(1)),
        ],
        out_specs=pl.BlockSpec((Q_ROWS, D_MODEL), lambda b, i: (b * qb + i, 0)),
        out_shape=jax.ShapeDtypeStruct((rows, D_MODEL), jnp.float32),
        scratch_shapes=[pltpu.VMEM((Q_ROWS, Q_WIDTH), jnp.bfloat16)],
        compiler_params=pltpu.CompilerParams(
            dimension_semantics=("arbitrary", "arbitrary"),
            vmem_limit_bytes=_vmem_limit(2 * seq * KV_WIDTH * 2 + wob.size * 2)),
        name="attention",
    )(q, kt, v, h, wob)


def _to_rows(x, meta):
    b, s, d = x.shape
    front = jnp.concatenate([jnp.zeros((META_PAD, d), x.dtype), meta.astype(x.dtype)], axis=0)
    rows = jnp.concatenate([jnp.broadcast_to(front[None], (b, FRONT, d)), x], axis=1)
    return rows.reshape(b * (FRONT + s), d)


def kernel(x_prompt, x_sample, meta_tokens, norm_mix_g, norm_ffn_g, ssd_w_in, ssd_conv_w, ssd_conv_b, ssd_dt_bias,
           ssd_a_log, ssd_d_skip, ssd_gate_norm_g, ssd_w_out, attn_w_qkv, attn_q_norm_g, attn_k_norm_g, attn_w_o,
           peer_w_q, peer_sub_keys, peer_u, peer_v):
    trunks = (x_prompt, x_sample)
    seq_rows = [FRONT + x.shape[1] for x in trunks for _ in range(x.shape[0])]
    trunk_rows = [x.shape[0] * (FRONT + x.shape[1]) for x in trunks]
    h = jnp.concatenate([_to_rows(x, meta_tokens) for x in trunks], axis=0)

    cos, sin = rope_tables(max(seq_rows))
    cos_all = jnp.concatenate([cos[:n] for n in seq_rows], axis=0)
    sin_all = jnp.concatenate([sin[:n] for n in seq_rows], axis=0)

    depth = norm_mix_g.shape[0]
    for i in range(depth):
        j = i // 2
        g_mix = norm_mix_g[i][None]
        if i % 2 == 0:
            h = ssd_layer(seq_rows, h, g_mix, ssd_w_in[j], ssd_conv_w[j], ssd_conv_b[j], ssd_dt_bias[j],
                          ssd_a_log[j], ssd_d_skip[j], ssd_gate_norm_g[j], ssd_w_out[j])
        else:
            q, kt, v = attn_in(h, g_mix, attn_w_qkv[j], attn_q_norm_g[j], attn_k_norm_g[j], cos_all, sin_all)
            parts, r0 = [], 0
            for x, n in zip(trunks, trunk_rows):
                r1 = r0 + n
                parts.append(attention(x.shape[0], q[r0:r1], kt[r0 // KEY_BLOCK:r1 // KEY_BLOCK], v[r0:r1],
                                       h[r0:r1], attn_w_o[j]))
                r0 = r1
            h = jnp.concatenate(parts, axis=0)
        wqt = peer_w_q[i].T.astype(jnp.bfloat16)
        keys = peer_sub_keys[i].reshape(2 * PEER_HEADS, N_KEYS, PEER_HALF).astype(jnp.bfloat16)
        h = peer_layer(h, norm_ffn_g[i][None], wqt, keys,
                       pack_expert_table(peer_u[i]), pack_expert_table(peer_v[i]))

    outs, r0 = [], 0
    for x, n in zip(trunks, trunk_rows):
        b, s, d = x.shape
        outs.append(h[r0:r0 + n].reshape(b, FRONT + s, d)[:, FRONT:])
        r0 += n
    return tuple(outs)
```
